```python
import math
import jax, jax.numpy as jnp
from jax import lax
import numpy as np

D_MODEL = 2048
BATCH = 2
SEQ = 4096
DEPTH = 4
DEC_BATCH = 128
DEC_SEQ = 4
PAST_LEN = 8192
PAGE_SIZE = 128

D_MIX = 2 * D_MODEL
MLA_WIDTH = D_MODEL
HG_WIDTH = D_MODEL // 2
RW_WIDTH = D_MODEL // 2
MLA_V_DIM = 128
MLA_HEADS = MLA_WIDTH // MLA_V_DIM
MLA_NOPE = 128
MLA_ROPE = 64
MLA_Q_RANK = D_MODEL // 4
MLA_KV_RANK = D_MODEL // 8
MLA_CACHE_DIM = MLA_KV_RANK + MLA_ROPE
MLA_SCALE = 1.0 / math.sqrt(MLA_NOPE + MLA_ROPE)
ROPE_BASE = 10000.0
Q_BLOCK = 128
NEG_BIG = -1e30
HG_DK = 128
HG_DV = 128
HG_HEADS = HG_WIDTH // HG_DV
HG_KEY = HG_HEADS * HG_DK
HG_CHUNK = 16
HG_F_MIN = 1e-30
RW_HEAD = 64
RW_HEADS = RW_WIDTH // RW_HEAD
RW_LORA_W = 64
RW_LORA_A = 64
RW_SHIFT = 3 * RW_WIDTH + RW_LORA_W + RW_LORA_A
GN_EPS = 64e-5
NORM_EPS = 1e-6
D_IN = MLA_Q_RANK + MLA_KV_RANK + MLA_ROPE + MLA_WIDTH + 2 * HG_KEY + 2 * HG_WIDTH + RW_SHIFT + RW_WIDTH

kernel_name = 'hymba_mla_hgrn2_rwkv7_step'


def _in_split_points():
    sizes = [MLA_Q_RANK, MLA_KV_RANK, MLA_ROPE, MLA_WIDTH,
             HG_KEY, HG_KEY, HG_WIDTH, HG_WIDTH,
             RW_SHIFT, RW_WIDTH]
    return [int(s) for s in np.cumsum(sizes)[:-1]]


def _rms_norm(x, g, eps=NORM_EPS):
    xf = x.astype(jnp.float32)
    y = xf * lax.rsqrt(jnp.mean(xf * xf, axis=-1, keepdims=True) + eps)
    return (y * g.astype(jnp.float32)).astype(x.dtype)


def _rope(x, pos):
    half = MLA_ROPE // 2
    inv = ROPE_BASE ** (-jnp.arange(half, dtype=jnp.float32) / half)
    ang = pos.astype(jnp.float32)[:, None] * inv[None, :]
    shape = (1, pos.shape[0]) + (1,) * (x.ndim - 3) + (half,)
    cos = jnp.cos(ang).reshape(shape)
    sin = jnp.sin(ang).reshape(shape)
    xf = x.astype(jnp.float32)
    x1, x2 = xf[..., :half], xf[..., half:]
    return jnp.concatenate([x1 * cos - x2 * sin, x1 * sin + x2 * cos], axis=-1).astype(x.dtype)


def _mla_prompt_attend(q_lat, q_pe, c_kv, k_pe):
    B, T, H, R = q_lat.shape
    blk = math.gcd(T, Q_BLOCK)
    nb = T // blk
    ql = q_lat.reshape(B, nb, blk, H, R).swapaxes(0, 1)
    qp = q_pe.reshape(B, nb, blk, H, MLA_ROPE).swapaxes(0, 1)
    key_pos = jnp.arange(T)

    def one_block(args):
        qlb, qpb, i = args
        s = (jnp.einsum('bqhr,bkr->bhqk', qlb, c_kv)
             + jnp.einsum('bqhe,bke->bhqk', qpb, k_pe)).astype(jnp.float32) * MLA_SCALE
        q_pos = i * blk + jnp.arange(blk)
        s = jnp.where((key_pos[None, :] <= q_pos[:, None])[None, None], s, NEG_BIG)
        p = jax.nn.softmax(s, axis=-1).astype(c_kv.dtype)
        return jnp.einsum('bhqk,bkr->bqhr', p, c_kv)

    o = lax.map(one_block, (ql, qp, jnp.arange(nb)))
    return o.swapaxes(0, 1).reshape(B, T, H, R)


def _mla_sample_attend(q_lat, q_pe, c_new, kpe_new, c_past, kpe_past):
    T = q_lat.shape[1]
    P = c_past.shape[1]
    s_past = (jnp.einsum('bqhr,bkr->bhqk', q_lat, c_past)
              + jnp.einsum('bqhe,bke->bhqk', q_pe, kpe_past)).astype(jnp.float32) * MLA_SCALE
    s_new = (jnp.einsum('bqhr,bkr->bhqk', q_lat, c_new)
             + jnp.einsum('bqhe,bke->bhqk', q_pe, kpe_new)).astype(jnp.float32) * MLA_SCALE
    causal = jnp.tril(jnp.ones((T, T), dtype=bool))
    s_new = jnp.where(causal[None, None], s_new, NEG_BIG)
    p = jax.nn.softmax(jnp.concatenate([s_past, s_new], axis=-1), axis=-1).astype(c_new.dtype)
    return (jnp.einsum('bhqk,bkr->bqhr', p[..., :P], c_past)
            + jnp.einsum('bhqk,bkr->bqhr', p[..., P:], c_new))


def _mla_branch(q_down, kv_down, k_rope_raw, pos, q_norm, kv_norm, w_q_up, w_kv_up, past):
    B, T, _ = q_down.shape
    q = jnp.einsum('btr,rhe->bthe', _rms_norm(q_down, q_norm), w_q_up)
    q_nope = q[..., :MLA_NOPE]
    q_pe = _rope(q[..., MLA_NOPE:], pos)
    c_kv = _rms_norm(kv_down, kv_norm)
    k_pe = _rope(k_rope_raw, pos)
    w_uk = w_kv_up[..., :MLA_NOPE]
    w_uv = w_kv_up[..., MLA_NOPE:]
    q_lat = jnp.einsum('bthn,rhn->bthr', q_nope, w_uk)
    if past is None:
        o_lat = _mla_prompt_attend(q_lat, q_pe, c_kv, k_pe)
    else:
        o_lat = _mla_sample_attend(q_lat, q_pe, c_kv, k_pe, past[0], past[1])
    o = jnp.einsum('bthr,rhv->bthv', o_lat, w_uv).reshape(B, T, MLA_WIDTH)
    rows = jnp.concatenate([c_kv, k_pe], axis=-1)
    return o, rows


def _hgrn2_chunked(q, k, v, log_f, s0):
    B, T, H, _ = q.shape
    C = math.gcd(T, HG_CHUNK)
    n = T // C

    def to_chunks(a):
        return a.astype(jnp.float32).reshape(B, n, C, H, a.shape[-1]).transpose(1, 0, 3, 2, 4)

    causal = jnp.tril(jnp.ones((C, C), dtype=bool))

    def step(S, inp):
        qc, kc, vc, gc = inp
        cum = jnp.cumsum(gc, axis=2)
        diff = jnp.where(causal[None, None, :, :, None],
                         cum[:, :, :, None, :] - cum[:, :, None, :, :], NEG_BIG)
        A = jnp.einsum('bhtd,bhsd,bhtsd->bhts', qc, kc, jnp.exp(diff))
        o = (jnp.einsum('bhts,bhsv->bhtv', A, vc)
             + jnp.einsum('bhtd,bhdv->bhtv', qc * jnp.exp(cum), S))
        last = cum[:, :, -1:, :]
        S = (jnp.exp(last[:, :, 0, :])[..., None] * S
             + jnp.einsum('bhsd,bhsv->bhdv', kc * jnp.exp(last - cum), vc))
        return S, o

    S, o = lax.scan(step, s0.astype(jnp.float32),
                    (to_chunks(q), to_chunks(k), to_chunks(v), to_chunks(log_f)))
    o = o.transpose(1, 0, 3, 2, 4).reshape(B, T, H, HG_DV)
    return o.astype(v.dtype), S.astype(s0.dtype)


def _hgrn2_branch(q_raw, f_raw, i_raw, lb, g_norm, s0):
    B, T, _ = q_raw.shape
    q = jax.nn.silu(q_raw).reshape(B, T, HG_HEADS, HG_DK)
    z = f_raw.astype(jnp.float32).reshape(B, T, HG_HEADS, HG_DK)
    lbh = lb.reshape(HG_HEADS, HG_DK).astype(jnp.float32)
    f = lbh + (1.0 - lbh) * jax.nn.sigmoid(z)
    log_f = jnp.log(jnp.maximum(f, HG_F_MIN))
    k = (1.0 - lbh) * jax.nn.sigmoid(-z)
    v = i_raw.reshape(B, T, HG_HEADS, HG_DV)
    o, s_new = _hgrn2_chunked(q, k, v, log_f, s0)
    o = _rms_norm(o, g_norm).reshape(B, T, HG_WIDTH)
    return o, s_new


def _rwkv7_branch(xs_raw, shift0, s0, p):
    B, T, _ = xs_raw.shape
    f32 = jnp.float32
    prev = jnp.concatenate([shift0[:, None].astype(xs_raw.dtype), xs_raw[:, :-1]], axis=1)
    xs = xs_raw + p['rw_mu'] * (prev - xs_raw)
    r, k, v, wl, al = jnp.split(
        xs, [RW_WIDTH, 2 * RW_WIDTH, 3 * RW_WIDTH, 3 * RW_WIDTH + RW_LORA_W], axis=-1)
    zw = (p['rw_w0'] + jnp.tanh(wl) @ p['rw_w_up']).astype(f32)
    log_w = -jnp.exp(-jax.nn.softplus(-zw) - 0.5)
    a = jax.nn.sigmoid((p['rw_a0'] + al @ p['rw_a_up']).astype(f32))

    def heads(t):
        return t.astype(f32).reshape(B, T, RW_HEADS, RW_HEAD)

    k_f = k.astype(f32)
    kk = heads(k_f * p['rw_k_k'])
    kk = kk * lax.rsqrt(jnp.maximum(jnp.sum(kk * kk, axis=-1, keepdims=True), 1e-24))
    k_f = k_f * (1.0 + (a - 1.0) * p['rw_k_a'])
    r_h, k_h, v_h, a_h, w_h = heads(r), heads(k_f), heads(v), heads(a), heads(log_w)

    def step(S, inp):
        rt, kt, vt, kkt, at, wt = inp
        sa = jnp.einsum('bhvk,bhk->bhv', S, -kkt)
        S = (S * jnp.exp(wt)[:, :, None, :]
             + sa[..., None] * (kkt * at)[:, :, None, :]
             + vt[..., None] * kt[:, :, None, :])
        return S, jnp.einsum('bhvk,bhk->bhv', S, rt)

    seq = tuple(t.swapaxes(0, 1) for t in (r_h, k_h, v_h, kk, a_h, w_h))
    S, y = lax.scan(step, s0.astype(f32), seq)
    y = y.swapaxes(0, 1)
    mean = jnp.mean(y, axis=-1, keepdims=True)
    var = jnp.mean(jnp.square(y - mean), axis=-1, keepdims=True)
    y = (y - mean) * lax.rsqrt(var + GN_EPS)
    y = (y * p['rw_gn_w'].astype(f32).reshape(RW_HEADS, RW_HEAD)
         + p['rw_gn_b'].astype(f32).reshape(RW_HEADS, RW_HEAD))
    y = y + jnp.sum(r_h * k_h * p['rw_r_k'].astype(f32), axis=-1, keepdims=True) * v_h
    return (y.reshape(B, T, RW_WIDTH).astype(xs_raw.dtype), S.astype(s0.dtype),
            xs_raw[:, -1].astype(shift0.dtype))


def _layer(x, pos, past, hg_s0, rw_s0, shift0, p):
    h = _rms_norm(x, p['pre_norm'])
    proj = jnp.einsum('btd,de->bte', h, p['w_in'])
    (q_down, kv_down, k_rope, g_mla, hg_q, hg_f, hg_i, g_hg, rw_x, g_rw) = jnp.split(
        proj, _in_split_points(), axis=-1)
    o_mla, rows = _mla_branch(q_down, kv_down, k_rope, pos, p['q_norm'], p['kv_norm'],
                              p['w_q_up'], p['w_kv_up'], past)
    o_hg, hg_s = _hgrn2_branch(hg_q, hg_f, hg_i, p['lb'], p['g_norm'], hg_s0)
    o_rw, rw_s, shift = _rwkv7_branch(rw_x, shift0, rw_s0, p)
    mixed = jnp.concatenate([o_mla * jax.nn.silu(g_mla),
                             o_hg * jax.nn.silu(g_hg),
                             o_rw * jax.nn.silu(g_rw)], axis=-1)
    y = x + _rms_norm(jnp.einsum('bte,ed->btd', mixed, p['w_out']), p['post_norm'])
    return y, rows, hg_s, rw_s, shift


def setup_inputs(seed: int = 0) -> dict:
    key = jax.random.key(seed)
    ks = jax.random.split(key, 32)
    f32 = jnp.float32
    n_pages = PAST_LEN // PAGE_SIZE
    n_used = DEC_BATCH * n_pages
    n_pool = n_used + n_used // 4

    def nrm(k, shape, scale):
        return jax.random.normal(k, shape, f32) * scale

    def gain(k, shape):
        return 1.0 + 0.05 * jax.random.normal(k, shape, f32)

    page_table = jax.random.permutation(ks[3], n_pool)[:n_used].reshape(DEC_BATCH, n_pages).astype(jnp.int32)
    return {
        'x_prompt': nrm(ks[0], (BATCH, SEQ, D_MODEL), 1.0),
        'x_sample': nrm(ks[1], (DEC_BATCH, DEC_SEQ, D_MODEL), 1.0),
        'cache_mla': nrm(ks[2], (DEPTH, n_pool, PAGE_SIZE, MLA_CACHE_DIM), 1.0),
        'page_table': page_table,
        'state_hgrn': nrm(ks[4], (DEPTH, DEC_BATCH, HG_HEADS, HG_DK, HG_DV), 0.5),
        'state_rwkv': nrm(ks[5], (DEPTH, DEC_BATCH, RW_HEADS, RW_HEAD, RW_HEAD), 0.3),
        'state_rwkv_shift': nrm(ks[6], (DEPTH, DEC_BATCH, RW_SHIFT), 1.0),
        'pre_norm': gain(ks[7], (DEPTH, D_MODEL)),
        'post_norm': gain(ks[8], (DEPTH, D_MODEL)),
        'w_in': nrm(ks[9], (DEPTH, D_MODEL, D_IN), D_MODEL ** -0.5),
        'w_out': nrm(ks[10], (DEPTH, D_MIX, D_MODEL), D_MIX ** -0.5),
        'mla_q_norm': gain(ks[11], (DEPTH, MLA_Q_RANK)),
        'mla_kv_norm': gain(ks[12], (DEPTH, MLA_KV_RANK)),
        'mla_w_q_up': nrm(ks[13], (DEPTH, MLA_Q_RANK, MLA_HEADS, MLA_NOPE + MLA_ROPE), MLA_Q_RANK ** -0.5),
        'mla_w_kv_up': nrm(ks[14], (DEPTH, MLA_KV_RANK, MLA_HEADS, MLA_NOPE + MLA_V_DIM), MLA_KV_RANK ** -0.5),
        'hg_lower_bound': nrm(ks[15], (DEPTH, HG_KEY), 0.1),
        'hg_g_norm': gain(ks[16], (DEPTH, HG_DV)),
        'rw_mu': jax.random.uniform(ks[17], (DEPTH, RW_SHIFT), f32),
        'rw_w0': nrm(ks[18], (DEPTH, RW_WIDTH), 0.5),
        'rw_w_up': nrm(ks[19], (DEPTH, RW_LORA_W, RW_WIDTH), RW_LORA_W ** -0.5),
        'rw_a0': nrm(ks[20], (DEPTH, RW_WIDTH), 0.1),
        'rw_a_up': nrm(ks[21], (DEPTH, RW_LORA_A, RW_WIDTH), RW_LORA_A ** -0.5),
        'rw_k_k': 0.85 + 0.05 * jax.random.normal(ks[22], (DEPTH, RW_WIDTH), f32),
        'rw_k_a': gain(ks[23], (DEPTH, RW_WIDTH)),
        'rw_r_k': nrm(ks[24], (DEPTH, RW_HEADS, RW_HEAD), 0.1),
        'rw_gn_w': gain(ks[25], (DEPTH, RW_WIDTH)),
        'rw_gn_b': nrm(ks[26], (DEPTH, RW_WIDTH), 0.02),
    }


def reference(x_prompt, x_sample, cache_mla, page_table, state_hgrn, state_rwkv, state_rwkv_shift,
              pre_norm, post_norm, w_in, w_out, mla_q_norm, mla_kv_norm, mla_w_q_up, mla_w_kv_up,
              hg_lower_bound, hg_g_norm, rw_mu, rw_w0, rw_w_up, rw_a0, rw_a_up, rw_k_k, rw_k_a,
              rw_r_k, rw_gn_w, rw_gn_b):
    B, T, _ = x_prompt.shape
    DB, DT, _ = x_sample.shape
    pos_p = jnp.arange(T)
    pos_s = PAST_LEN + jnp.arange(DT)
    lb_sm = jax.nn.softmax(hg_lower_bound.astype(jnp.float32), axis=0)
    lbs = jnp.cumsum(lb_sm, axis=0) - lb_sm[0]

    xp, xs = x_prompt, x_sample
    rows_p, rows_s, hg_p, hg_s, rw_p, rw_s, sh_p, sh_s = [], [], [], [], [], [], [], []
    for l in range(DEPTH):
        p = {
            'pre_norm': pre_norm[l], 'post_norm': post_norm[l], 'w_in': w_in[l], 'w_out': w_out[l],
            'q_norm': mla_q_norm[l], 'kv_norm': mla_kv_norm[l],
            'w_q_up': mla_w_q_up[l], 'w_kv_up': mla_w_kv_up[l],
            'lb': lbs[l], 'g_norm': hg_g_norm[l],
            'rw_mu': rw_mu[l], 'rw_w0': rw_w0[l], 'rw_w_up': rw_w_up[l], 'rw_a0': rw_a0[l],
            'rw_a_up': rw_a_up[l], 'rw_k_k': rw_k_k[l], 'rw_k_a': rw_k_a[l], 'rw_r_k': rw_r_k[l],
            'rw_gn_w': rw_gn_w[l], 'rw_gn_b': rw_gn_b[l],
        }
        hg0 = jnp.zeros((B, HG_HEADS, HG_DK, HG_DV), state_hgrn.dtype)
        rw0 = jnp.zeros((B, RW_HEADS, RW_HEAD, RW_HEAD), state_rwkv.dtype)
        sh0 = jnp.zeros((B, RW_SHIFT), state_rwkv_shift.dtype)
        xp, r_p, h_p, w_p, s_p = _layer(xp, pos_p, None, hg0, rw0, sh0, p)
        past = cache_mla[l, page_table].reshape(DB, -1, MLA_CACHE_DIM).astype(xs.dtype)
        past_pair = (past[..., :MLA_KV_RANK], past[..., MLA_KV_RANK:])
        xs, r_s, h_s, w_s, s_s = _layer(xs, pos_s, past_pair, state_hgrn[l], state_rwkv[l],
                                        state_rwkv_shift[l], p)
        rows_p.append(r_p.astype(cache_mla.dtype)); rows_s.append(r_s.astype(cache_mla.dtype))
        hg_p.append(h_p); hg_s.append(h_s)
        rw_p.append(w_p); rw_s.append(w_s)
        sh_p.append(s_p); sh_s.append(s_s)
    return (xp, xs, jnp.stack(rows_p), jnp.stack(rows_s), jnp.stack(hg_p), jnp.stack(hg_s),
            jnp.stack(rw_p), jnp.stack(rw_s), jnp.stack(sh_p), jnp.stack(sh_s))
```

```python
import functools
import math

import jax
import jax.numpy as jnp
import numpy as np
from jax import lax
from jax.experimental import pallas as pl
from jax.experimental.pallas import tpu as pltpu

F32 = jnp.float32
BF16 = jnp.bfloat16

D_MODEL = 2048
DEPTH = 4
PAGE = 128
MLA_HEADS = 16
MLA_NOPE = 128
MLA_ROPE = 64
MLA_V = 128
MLA_QR = 512
MLA_KVR = 256
MLA_CACHE = MLA_KVR + MLA_ROPE
MLA_QK = MLA_KVR + 128
MLA_SCALE = 1.0 / math.sqrt(MLA_NOPE + MLA_ROPE)
ROPE_BASE = 10000.0
NEG_BIG = -1e30
HG_HEADS = 8
HG_D = 128
HG_F_MIN = 1e-30
RW_HEADS = 16
RW_HEAD = 64
RW_WIDTH = 1024
RW_LORA = 64
RW_SHIFT = 3 * RW_WIDTH + 2 * RW_LORA
RW_SHIFT_PAD = 3 * RW_WIDTH + 256
GN_EPS = 64e-5
NORM_EPS = 1e-6
SAMPLE_PAD = 8
VMEM_LIMIT = 48 * 1024 * 1024

COL_SMALL, COL_HGQ, COL_GMLA, COL_HGF, COL_HGI, COL_GHG, COL_GRW, COL_RWR, COL_RWK, COL_RWV = (
    0, 1, 2, 4, 5, 6, 7, 8, 9, 10)
D_IN_PAD = 11 * 1024


def _cparams(sem):
    return pltpu.CompilerParams(dimension_semantics=sem, vmem_limit_bytes=VMEM_LIMIT)


def _dot(a, b):
    return jnp.dot(a.astype(BF16), b.astype(BF16), preferred_element_type=F32)


def _dot_nt(a, b):
    return lax.dot_general(a.astype(BF16), b.astype(BF16), (((1,), (1,)), ((), ())),
                           preferred_element_type=F32)


def _dot_tn(a, b):
    return lax.dot_general(a.astype(BF16), b.astype(BF16), (((0,), (0,)), ((), ())),
                           preferred_element_type=F32)


def _dot_split(m01, x):
    m = m01.astype(BF16)
    hi = x.astype(BF16)
    r1 = x - hi.astype(F32)
    mid = r1.astype(BF16)
    lo = (r1 - mid.astype(F32)).astype(BF16)
    d = lambda p: jnp.dot(m, p, preferred_element_type=F32)
    return d(hi) + d(mid) + d(lo)


def _dot_split_r(x, m01):
    m = m01.astype(BF16)
    hi = x.astype(BF16)
    r1 = x - hi.astype(F32)
    mid = r1.astype(BF16)
    lo = (r1 - mid.astype(F32)).astype(BF16)
    d = lambda p: jnp.dot(p, m, preferred_element_type=F32)
    return d(hi) + d(mid) + d(lo)


def _silu(x):
    return x * jax.nn.sigmoid(x)


def _iota(shape, dim):
    return lax.broadcasted_iota(jnp.int32, shape, dim)


def _rms_cast_kernel(x_ref, g_ref, o_ref):
    x = x_ref[...]
    y = x * lax.rsqrt(jnp.mean(x * x, axis=-1, keepdims=True) + NORM_EPS)
    o_ref[...] = (y * g_ref[...]).astype(BF16)


def _rms_cast(x, g, tm=512):
    n, d = x.shape
    return pl.pallas_call(
        _rms_cast_kernel,
        grid=(n // tm,),
        in_specs=[pl.BlockSpec((tm, d), lambda i: (i, 0)),
                  pl.BlockSpec((1, d), lambda i: (0, 0))],
        out_specs=pl.BlockSpec((tm, d), lambda i: (i, 0)),
        out_shape=jax.ShapeDtypeStruct((n, d), BF16),
        compiler_params=_cparams(("parallel",)),
        name="rms_cast",
    )(x, g.reshape(1, d))


def _matmul_kernel(a_ref, b_ref, o_ref):
    o_ref[...] = jnp.dot(a_ref[...], b_ref[...], preferred_element_type=F32)


def _matmul(a, b, tm=512, tn=1024):
    m, k = a.shape
    _, n = b.shape
    return pl.pallas_call(
        _matmul_kernel,
        grid=(n // tn, m // tm),
        in_specs=[pl.BlockSpec((tm, k), lambda j, i: (i, 0)),
                  pl.BlockSpec((k, tn), lambda j, i: (0, j))],
        out_specs=pl.BlockSpec((tm, tn), lambda j, i: (i, j)),
        out_shape=jax.ShapeDtypeStruct((m, n), F32),
        compiler_params=_cparams(("parallel", "parallel")),
        name="in_proj",
    )(a, b)


def _mla_prep_kernel(p_ref, pos_ref, inv_ref, qn_ref, kvn_ref, wq_ref, wuk_ref,
                     qc_ref, rows_ref, kc_ref):
    x = p_ref[...]
    qd = x[:, 0:MLA_QR]
    kvd = x[:, MLA_QR:MLA_QR + MLA_KVR]
    kr = x[:, 768:832]
    krsw = x[:, 896:960]
    qn = qd * lax.rsqrt(jnp.mean(qd * qd, axis=-1, keepdims=True) + NORM_EPS) * qn_ref[...]
    q = _dot(qn, wq_ref[...])
    ang = pos_ref[...] * inv_ref[...]
    cos = jnp.cos(ang)
    sin = jnp.sin(ang)
    hw = MLA_HEADS * 128
    for h in range(MLA_HEADS):
        ql = _dot(q[:, 128 * h:128 * (h + 1)], wuk_ref[h])
        pe = (q[:, hw + 128 * h:hw + 128 * (h + 1)] * cos
              + q[:, 2 * hw + 128 * h:2 * hw + 128 * (h + 1)] * sin)
        qc_ref[h, :, 0:MLA_KVR] = (ql * MLA_SCALE).astype(BF16)
        qc_ref[h, :, MLA_KVR:MLA_QK] = (pe * MLA_SCALE).astype(BF16)
    ckv = kvd * lax.rsqrt(jnp.mean(kvd * kvd, axis=-1, keepdims=True) + NORM_EPS) * kvn_ref[...]
    kpe = kr * cos[:, 0:64] + krsw * sin[:, 0:64]
    rows_ref[:, 0:MLA_KVR] = ckv
    rows_ref[:, MLA_KVR:MLA_CACHE] = kpe
    kc_ref[:, 0:MLA_KVR] = ckv.astype(BF16)
    kc_ref[:, MLA_KVR:MLA_CACHE] = kpe.astype(BF16)
    kc_ref[:, MLA_CACHE:MLA_QK] = jnp.zeros((x.shape[0], MLA_QK - MLA_CACHE), BF16)


def _mla_prep(proj, pos, inv128, q_norm, kv_norm, wq, wuk, tq=256):
    n = proj.shape[0]
    return pl.pallas_call(
        _mla_prep_kernel,
        grid=(n // tq,),
        in_specs=[pl.BlockSpec((tq, 1024), lambda i: (i, COL_SMALL)),
                  pl.BlockSpec((tq, 1), lambda i: (i, 0)),
                  pl.BlockSpec((1, 128), lambda i: (0, 0)),
                  pl.BlockSpec((1, MLA_QR), lambda i: (0, 0)),
                  pl.BlockSpec((1, MLA_KVR), lambda i: (0, 0)),
                  pl.BlockSpec(wq.shape, lambda i: (0, 0)),
                  pl.BlockSpec(wuk.shape, lambda i: (0, 0, 0))],
        out_specs=[pl.BlockSpec((MLA_HEADS, tq, MLA_QK), lambda i: (0, i, 0)),
                   pl.BlockSpec((tq, MLA_CACHE), lambda i: (i, 0)),
                   pl.BlockSpec((tq, MLA_QK), lambda i: (i, 0))],
        out_shape=[jax.ShapeDtypeStruct((MLA_HEADS, n, MLA_QK), BF16),
                   jax.ShapeDtypeStruct((n, MLA_CACHE), F32),
                   jax.ShapeDtypeStruct((n, MLA_QK), BF16)],
        compiler_params=_cparams(("parallel",)),
        name="mla_prep",
    )(proj, pos, inv128, q_norm.reshape(1, -1), kv_norm.reshape(1, -1), wq, wuk)


def _attn_prompt_kernel(q_ref, k_ref, g_ref, wuv_ref, o_ref, m_sc, l_sc, acc_sc, *, tq, tk):
    i = pl.program_id(1)
    j = pl.program_id(2)
    last = (i * tq + tq - 1) // tk
    rows = MLA_HEADS * tq

    @pl.when(j == 0)
    def _():
        m_sc[...] = jnp.full(m_sc.shape, NEG_BIG, F32)
        l_sc[...] = jnp.zeros(l_sc.shape, F32)
        acc_sc[...] = jnp.zeros(acc_sc.shape, F32)

    @pl.when(j <= last)
    def _():
        q = q_ref[...].reshape(rows, MLA_QK)
        k = k_ref[...]
        s = lax.dot_general(q, k, (((1,), (1,)), ((), ())), preferred_element_type=F32)
        qpos = i * tq + (_iota((rows, 1), 0) & (tq - 1))
        kpos = j * tk + _iota((1, tk), 1)
        s = jnp.where(kpos <= qpos, s, NEG_BIG)
        m_prev = m_sc[...]
        m_new = jnp.maximum(m_prev, jnp.max(s, axis=-1, keepdims=True))
        alpha = jnp.exp(m_prev - m_new)
        p = jnp.exp(s - m_new)
        l_sc[...] = alpha * l_sc[...] + jnp.sum(p, axis=-1, keepdims=True)
        acc_sc[...] = alpha * acc_sc[...] + jnp.dot(p.astype(BF16), k[:, 0:MLA_KVR],
                                                    preferred_element_type=F32)
        m_sc[...] = m_new

    @pl.when(j == last)
    def _():
        o_lat = acc_sc[...] / l_sc[...]
        g = g_ref[...]
        for h in range(MLA_HEADS):
            oh = _dot(o_lat[h * tq:(h + 1) * tq, :], wuv_ref[h])
            o_ref[:, 128 * h:128 * (h + 1)] = (oh * _silu(g[:, 128 * h:128 * (h + 1)])).astype(BF16)


def _attn_prompt(qc, kc, proj, wuv, nb, t, tq=128, tk=512):
    nq = t // tq
    nk = t // tk
    kern = functools.partial(_attn_prompt_kernel, tq=tq, tk=tk)

    def kmap(b, i, j):
        return (b * nk + jnp.minimum(j, (i * tq + tq - 1) // tk), 0)

    return pl.pallas_call(
        kern,
        grid=(nb, nq, nk),
        in_specs=[pl.BlockSpec((MLA_HEADS, tq, MLA_QK), lambda b, i, j: (0, b * nq + i, 0)),
                  pl.BlockSpec((tk, MLA_QK), kmap),
                  pl.BlockSpec((tq, 2048), lambda b, i, j: (b * nq + i, COL_GMLA // 2)),
                  pl.BlockSpec(wuv.shape, lambda b, i, j: (0, 0, 0))],
        out_specs=pl.BlockSpec((tq, 2048), lambda b, i, j: (b * nq + i, 0)),
        out_shape=jax.ShapeDtypeStruct((nb * t, 2048), BF16),
        scratch_shapes=[pltpu.VMEM((MLA_HEADS * tq, 1), F32),
                        pltpu.VMEM((MLA_HEADS * tq, 1), F32),
                        pltpu.VMEM((MLA_HEADS * tq, MLA_KVR), F32)],
        compiler_params=_cparams(("parallel", "parallel", "arbitrary")),
        name="attn_prompt",
    )(qc, kc, proj, wuv)


def _attn_sample_kernel(pt_ref, q_ref, nk_ref, *rest, layer, nseq, npg, dt):
    pages = rest[:nseq * npg]
    o_ref = rest[nseq * npg]
    m_sc, l_sc, acc_sc = rest[nseq * npg + 1:]
    j = pl.program_id(1)
    rows = MLA_HEADS * dt

    @pl.when(j == 0)
    def _():
        for s in range(nseq):
            q = q_ref[s]
            kn = nk_ref[s]
            sc = lax.dot_general(q, kn, (((1,), (1,)), ((), ())), preferred_element_type=F32)
            trow = _iota((rows, 1), 0) & (dt - 1)
            kcol = _iota((1, kn.shape[0]), 1)
            sc = jnp.where(kcol <= trow, sc, NEG_BIG)
            m = jnp.max(sc, axis=-1, keepdims=True)
            p = jnp.exp(sc - m)
            m_sc[s] = m
            l_sc[s] = jnp.sum(p, axis=-1, keepdims=True)
            acc_sc[s] = jnp.dot(p.astype(BF16), kn[:, 0:MLA_KVR], preferred_element_type=F32)

    for s in range(nseq):
        q = q_ref[s][:, 0:MLA_CACHE]
        kp = jnp.concatenate([pages[s * npg + g][...] for g in range(npg)], axis=0)
        kb = kp.astype(BF16)
        sc = lax.dot_general(q, kb, (((1,), (1,)), ((), ())), preferred_element_type=F32)
        m_prev = m_sc[s]
        m_new = jnp.maximum(m_prev, jnp.max(sc, axis=-1, keepdims=True))
        alpha = jnp.exp(m_prev - m_new)
        p = jnp.exp(sc - m_new)
        l_sc[s] = alpha * l_sc[s] + jnp.sum(p, axis=-1, keepdims=True)
        acc_sc[s] = alpha * acc_sc[s] + jnp.dot(p.astype(BF16), kb[:, 0:MLA_KVR],
                                                preferred_element_type=F32)
        m_sc[s] = m_new

    @pl.when(j == pl.num_programs(1) - 1)
    def _():
        for s in range(nseq):
            o_ref[s] = acc_sc[s] / l_sc[s]


def _attn_sample(page_table_flat, q_s, newk, cache, layer, n_pages, dt, nseq=2, npg=8):
    db = q_s.shape[0]
    rows = MLA_HEADS * dt
    ngrp = n_pages // npg
    kern = functools.partial(_attn_sample_kernel, layer=layer, nseq=nseq, npg=npg, dt=dt)

    def page_spec(s, g):
        def imap(bb, j, pt):
            return (layer, pt[(bb * nseq + s) * n_pages + j * npg + g], 0, 0)
        return pl.BlockSpec((None, None, PAGE, MLA_CACHE), imap)

    in_specs = [pl.BlockSpec((nseq, rows, MLA_QK), lambda bb, j, pt: (bb, 0, 0)),
                pl.BlockSpec((nseq, 16, MLA_QK), lambda bb, j, pt: (bb, 0, 0))]
    in_specs += [page_spec(s, g) for s in range(nseq) for g in range(npg)]
    grid_spec = pltpu.PrefetchScalarGridSpec(
        num_scalar_prefetch=1,
        grid=(db // nseq, ngrp),
        in_specs=in_specs,
        out_specs=pl.BlockSpec((nseq, rows, MLA_KVR), lambda bb, j, pt: (bb, 0, 0)),
        scratch_shapes=[pltpu.VMEM((nseq, rows, 1), F32),
                        pltpu.VMEM((nseq, rows, 1), F32),
                        pltpu.VMEM((nseq, rows, MLA_KVR), F32)],
    )
    return pl.pallas_call(
        kern,
        grid_spec=grid_spec,
        out_shape=jax.ShapeDtypeStruct((db, rows, MLA_KVR), F32),
        compiler_params=_cparams(("parallel", "arbitrary")),
        name="attn_sample",
    )(page_table_flat, q_s, newk, *([cache] * (nseq * npg)))


def _oproj_gate_kernel(o_ref, g_ref, wuv_ref, out_ref):
    g = g_ref[...]
    for h in range(MLA_HEADS):
        oh = _dot(o_ref[h], wuv_ref[h])
        out_ref[:, 128 * h:128 * (h + 1)] = (oh * _silu(g[:, 128 * h:128 * (h + 1)])).astype(BF16)


def _oproj_gate(o_lat, proj, wuv, tm=256):
    n = o_lat.shape[1]
    return pl.pallas_call(
        _oproj_gate_kernel,
        grid=(n // tm,),
        in_specs=[pl.BlockSpec((MLA_HEADS, tm, MLA_KVR), lambda i: (0, i, 0)),
                  pl.BlockSpec((tm, 2048), lambda i: (i, COL_GMLA // 2)),
                  pl.BlockSpec(wuv.shape, lambda i: (0, 0, 0))],
        out_specs=pl.BlockSpec((tm, 2048), lambda i: (i, 0)),
        out_shape=jax.ShapeDtypeStruct((n, 2048), BF16),
        compiler_params=_cparams(("parallel",)),
        name="oproj_gate",
    )(o_lat, proj, wuv)


def _hgrn_kernel(lb_ref, gn_ref, q_ref, f_ref, i_ref, g_ref, *rest, layer, c, valid, has_init):
    if has_init:
        s0_ref, o_ref, sout_ref, st_sc = rest
    else:
        o_ref, sout_ref, st_sc = rest
    ci = pl.program_id(1)

    @pl.when(ci == 0)
    def _():
        if has_init:
            for h in range(HG_HEADS):
                st_sc[h] = s0_ref[0, h].T
        else:
            st_sc[...] = jnp.zeros(st_sc.shape, F32)

    lbx = lb_ref[...]
    e = jnp.exp(lbx - jnp.max(lbx, axis=0, keepdims=True))
    sm = e / jnp.sum(e, axis=0, keepdims=True)
    lb = jnp.zeros((1, lbx.shape[1]), F32)
    for l in range(1, layer + 1):
        lb = lb + sm[l:l + 1, :]

    z = f_ref[...]
    q_all = _silu(q_ref[...])
    f = lb + (1.0 - lb) * jax.nn.sigmoid(z)
    logf_all = jnp.log(jnp.maximum(f, HG_F_MIN))
    k_all = (1.0 - lb) * jax.nn.sigmoid(-z)
    v_all = i_ref[...]
    t_col = _iota((c, 1), 0)
    if valid < c:
        live = t_col < valid
        logf_all = jnp.where(live, logf_all, 0.0)
        k_all = jnp.where(live, k_all, 0.0)

    t_r = _iota((c, c), 0)
    t_c = _iota((c, c), 1)
    tri = (t_c <= t_r).astype(F32)
    m4 = t_r & 3
    lvl2 = (((m4 == 2) & (t_c == t_r)) | ((m4 == 3) & ((t_c == t_r) | (t_c == t_r - 1)))
            | ((m4 == 0) & (t_c == t_r + 1))).astype(F32)
    cum_all = _dot_split(tri, logf_all)
    e2_all = _dot_split(lvl2, logf_all) if c >= 4 else None

    for h in range(HG_HEADS):
        sl = slice(128 * h, 128 * (h + 1))
        q, k, v, logf, cum = q_all[:, sl], k_all[:, sl], v_all[:, sl], logf_all[:, sl], cum_all[:, sl]
        a = jnp.zeros((c, c), F32)
        half = c // 2
        while half >= 1:
            upper = (t_col & (2 * half - 1)) >= half
            if half >= 4:
                nblk = c // (2 * half)
                c3 = cum.reshape(nblk, 2 * half, 128)
                ref = jnp.broadcast_to(c3[:, half - 1:half, :], (nblk, 2 * half, 128)).reshape(c, 128)
                eq = jnp.where(upper, cum - ref, NEG_BIG)
                ek = jnp.where(upper, NEG_BIG, ref - cum)
            elif half == 2:
                e2 = e2_all[:, sl]
                eq = jnp.where(upper, e2, NEG_BIG)
                ek = jnp.where(upper, NEG_BIG, e2)
            else:
                eq = jnp.where(upper, logf, NEG_BIG)
                ek = jnp.where(upper, NEG_BIG, 0.0)
            prod = _dot_nt(q * jnp.exp(eq), k * jnp.exp(ek))
            shift = int(math.log2(2 * half))
            a = a + jnp.where((t_r >> shift) == (t_c >> shift), prod, 0.0)
            half //= 2
        a = jnp.where(t_r == t_c, jnp.sum(q * k, axis=-1, keepdims=True), a)
        st = st_sc[h]
        o = _dot(a, v) + _dot_nt(q * jnp.exp(cum), st)
        last = cum[c - 1:c, :]
        st_sc[h] = st * jnp.exp(last) + _dot_tn(v, k * jnp.exp(last - cum))
        o = o * lax.rsqrt(jnp.mean(o * o, axis=-1, keepdims=True) + NORM_EPS) * gn_ref[...]
        o_ref[:, sl] = (o * _silu(g_ref[:, sl])).astype(o_ref.dtype)

    @pl.when(ci == pl.num_programs(1) - 1)
    def _():
        for h in range(HG_HEADS):
            sout_ref[0, h] = st_sc[h].T


def _hgrn(proj, lb_raw, g_norm, s0, layer, nseq, t, c, valid):
    nc = t // c
    has_init = s0 is not None
    kern = functools.partial(_hgrn_kernel, layer=layer, c=c, valid=valid, has_init=has_init)

    def col(cb):
        return pl.BlockSpec((c, 1024), lambda b, ci: (b * nc + ci, cb))

    in_specs = [pl.BlockSpec((DEPTH, 1024), lambda b, ci: (0, 0)),
                pl.BlockSpec((1, HG_D), lambda b, ci: (0, 0)),
                col(COL_HGQ), col(COL_HGF), col(COL_HGI), col(COL_GHG)]
    args = [lb_raw, g_norm.reshape(1, HG_D), proj, proj, proj, proj]
    if has_init:
        in_specs.append(pl.BlockSpec((1, HG_HEADS, HG_D, HG_D), lambda b, ci: (b, 0, 0, 0)))
        args.append(s0)
    return pl.pallas_call(
        kern,
        grid=(nseq, nc),
        in_specs=in_specs,
        out_specs=[pl.BlockSpec((c, 1024), lambda b, ci: (b * nc + ci, 0)),
                   pl.BlockSpec((1, HG_HEADS, HG_D, HG_D), lambda b, ci: (b, 0, 0, 0))],
        out_shape=[jax.ShapeDtypeStruct((nseq * t, 1024), BF16 if c % 16 == 0 else F32),
                   jax.ShapeDtypeStruct((nseq, HG_HEADS, HG_D, HG_D), F32)],
        scratch_shapes=[pltpu.VMEM((HG_HEADS, HG_D, HG_D), F32)],
        compiler_params=_cparams(("parallel", "arbitrary")),
        name="hgrn",
    )(*args)


def _rwkv_kernel(xr_ref, xk_ref, xv_ref, xs_ref, g_ref, mu_ref, w0_ref, wup_ref, a0_ref, aup_ref,
                 kk_ref, ka_ref, rk_ref, gnw_ref, gnb_ref, *rest, c, valid, has_init):
    if has_init:
        sh0_ref, s0_ref, o_ref, sout_ref, s_sc, carry_sc = rest
    else:
        o_ref, sout_ref, s_sc, carry_sc = rest
    ci = pl.program_id(1)
    npair = RW_HEADS // 2

    @pl.when(ci == 0)
    def _():
        if has_init:
            carry_sc[0:1, :] = sh0_ref[0]
            z64 = jnp.zeros((RW_HEAD, RW_HEAD), F32)
            for p in range(npair):
                top = jnp.concatenate([s0_ref[0, 2 * p], z64], axis=1)
                bot = jnp.concatenate([z64, s0_ref[0, 2 * p + 1]], axis=1)
                s_sc[p] = jnp.concatenate([top, bot], axis=0)
        else:
            carry_sc[...] = jnp.zeros(carry_sc.shape, F32)
            s_sc[...] = jnp.zeros(s_sc.shape, F32)

    t_col = _iota((c, 1), 0)
    first = t_col == 0
    w = RW_WIDTH

    def shifted(x, lo, hi):
        prev = jnp.where(first, carry_sc[0:1, lo:hi], pltpu.roll(x, 1, 0))
        return x + mu_ref[:, lo:hi] * (prev - x)

    xr, xk, xv = xr_ref[...], xk_ref[...], xv_ref[...]
    xs = xs_ref[...]
    xw = xs[:, 768:896]
    xa = xs[:, 896:1024]
    r = shifted(xr, 0, w)
    k = shifted(xk, w, 2 * w)
    v = shifted(xv, 2 * w, 3 * w)
    wl = shifted(xw, 3 * w, 3 * w + 128)
    al = shifted(xa, 3 * w + 128, 3 * w + 256)
    if valid == c:
        carry_sc[0:1, 0:w] = xr[c - 1:c, :]
        carry_sc[0:1, w:2 * w] = xk[c - 1:c, :]
        carry_sc[0:1, 2 * w:3 * w] = xv[c - 1:c, :]
        carry_sc[0:1, 3 * w:3 * w + 128] = xw[c - 1:c, :]
        carry_sc[0:1, 3 * w + 128:3 * w + 256] = xa[c - 1:c, :]

    zw = w0_ref[...] + _dot(jnp.tanh(wl), wup_ref[...])
    log_w = -math.exp(-0.5) * jax.nn.sigmoid(zw)
    ag = jax.nn.sigmoid(a0_ref[...] + _dot(al, aup_ref[...]))

    lane = _iota((128, 128), 0) >> 6
    seg = (lane == (_iota((128, 128), 1) >> 6)).astype(F32)

    def seg_sum(x):
        return jnp.concatenate(
            [_dot_split_r(x[:, 128 * p:128 * (p + 1)], seg) for p in range(npair)], axis=1)

    kk = k * kk_ref[...]
    kk = kk * lax.rsqrt(jnp.maximum(seg_sum(kk * kk), 1e-24))
    kf = k * (1.0 + (ag - 1.0) * ka_ref[...])
    if valid < c:
        live = t_col < valid
        kk = jnp.where(live, kk, 0.0)
        kf = jnp.where(live, kf, 0.0)
        log_w = jnp.where(live, log_w, 0.0)

    t_r = _iota((c, c), 0)
    t_c = _iota((c, c), 1)
    tri = (t_c <= t_r).astype(F32)
    strict = t_c < t_r
    incl = t_c <= t_r
    eye = (t_c == t_r).astype(F32)
    cum = _dot_split(tri, log_w)
    e_in = jnp.exp(cum)
    e_out = jnp.exp(-cum)
    at_all = -kk * jnp.exp(cum - log_w)
    bt_all = kk * ag * e_out
    kt_all = kf * e_out
    rt_all = r * e_in
    pc_all = e_in[c - 1:c, :]
    head_lo = _iota((1, 128), 1) < RW_HEAD
    nsteps = 0
    cover = 2
    while cover < valid:
        cover *= 2
        nsteps += 1

    y_parts = []
    for p in range(npair):
        sl = slice(128 * p, 128 * (p + 1))
        at, bt, kt, rt, vp, pc = at_all[:, sl], bt_all[:, sl], kt_all[:, sl], rt_all[:, sl], v[:, sl], pc_all[:, sl]
        s_bd = s_sc[p]
        u = jnp.zeros((c, 128), F32)
        y = _dot_nt(rt, s_bd)
        for hh in range(2):
            hm = head_lo if hh == 0 else jnp.logical_not(head_lo)
            at_h = jnp.where(hm, at, 0.0)
            rt_h = jnp.where(hm, rt, 0.0)
            lab = jnp.where(strict, _dot_nt(at_h, bt), 0.0)
            lak = jnp.where(strict, _dot_nt(at_h, kt), 0.0)
            rhs = _dot_nt(at_h, s_bd) + jnp.where(hm, _dot(lak, vp), 0.0)
            x = eye + lab
            pw = lab
            for _ in range(nsteps):
                pw = _dot(pw, pw)
                x = x + _dot(x, pw)
            u_h = _dot(x, rhs)
            mrb = jnp.where(incl, _dot_nt(rt_h, bt), 0.0)
            mrk = jnp.where(incl, _dot_nt(rt_h, kt), 0.0)
            y = y + _dot(mrb, u_h) + jnp.where(hm, _dot(mrk, vp), 0.0)
            u = u + u_h
        s_new = s_bd * pc + _dot_tn(u, bt * pc) + _dot_tn(vp, kt * pc)
        s_sc[p] = jnp.where(seg > 0.0, s_new, 0.0)
        y_parts.append(y)
    y = jnp.concatenate(y_parts, axis=1)

    mean = seg_sum(y) * (1.0 / RW_HEAD)
    d = y - mean
    var = seg_sum(d * d) * (1.0 / RW_HEAD)
    yn = d * lax.rsqrt(var + GN_EPS) * gnw_ref[...] + gnb_ref[...]
    yn = yn + seg_sum(r * kf * rk_ref[...]) * v
    o_ref[...] = (yn * _silu(g_ref[...])).astype(o_ref.dtype)

    @pl.when(ci == pl.num_programs(1) - 1)
    def _():
        for p in range(npair):
            s_bd = s_sc[p]
            sout_ref[0, 2 * p] = s_bd[0:RW_HEAD, 0:RW_HEAD]
            sout_ref[0, 2 * p + 1] = s_bd[RW_HEAD:128, RW_HEAD:128]


def _rwkv(proj, prm, sh0, s0, nseq, t, c, valid):
    nc = t // c
    has_init = s0 is not None
    kern = functools.partial(_rwkv_kernel, c=c, valid=valid, has_init=has_init)

    def col(cb):
        return pl.BlockSpec((c, 1024), lambda b, ci: (b * nc + ci, cb))

    def const(shape):
        return pl.BlockSpec(shape, lambda b, ci: (0,) * len(shape))

    in_specs = [col(COL_RWR), col(COL_RWK), col(COL_RWV), col(COL_SMALL), col(COL_GRW),
                const((1, RW_SHIFT_PAD)), const((1, 1024)), const((128, 1024)), const((1, 1024)),
                const((128, 1024)), const((1, 1024)), const((1, 1024)), const((1, 1024)),
                const((1, 1024)), const((1, 1024))]
    args = [proj, proj, proj, proj, proj, prm["mu"], prm["w0"], prm["w_up"], prm["a0"], prm["a_up"],
            prm["k_k"], prm["k_a"], prm["r_k"], prm["gn_w"], prm["gn_b"]]
    if has_init:
        in_specs += [pl.BlockSpec((1, 1, RW_SHIFT_PAD), lambda b, ci: (b, 0, 0)),
                     pl.BlockSpec((1, RW_HEADS, RW_HEAD, RW_HEAD), lambda b, ci: (b, 0, 0, 0))]
        args += [sh0, s0]
    return pl.pallas_call(
        kern,
        grid=(nseq, nc),
        in_specs=in_specs,
        out_specs=[pl.BlockSpec((c, 1024), lambda b, ci: (b * nc + ci, 0)),
                   pl.BlockSpec((1, RW_HEADS, RW_HEAD, RW_HEAD), lambda b, ci: (b, 0, 0, 0))],
        out_shape=[jax.ShapeDtypeStruct((nseq * t, 1024), BF16 if c % 16 == 0 else F32),
                   jax.ShapeDtypeStruct((nseq, RW_HEADS, RW_HEAD, RW_HEAD), F32)],
        scratch_shapes=[pltpu.VMEM((RW_HEADS // 2, 128, 128), F32),
                        pltpu.VMEM((8, RW_SHIFT_PAD), F32)],
        compiler_params=_cparams(("parallel", "arbitrary")),
        name="rwkv",
    )(*args)


def _out_proj_kernel(x_ref, m_ref, hg_ref, rw_ref, w_ref, g_ref, o_ref):
    acc = _dot(m_ref[...], w_ref[0:2048, :])
    acc = acc + _dot(hg_ref[...], w_ref[2048:3072, :])
    acc = acc + _dot(rw_ref[...], w_ref[3072:4096, :])
    y = acc * lax.rsqrt(jnp.mean(acc * acc, axis=-1, keepdims=True) + NORM_EPS) * g_ref[...]
    o_ref[...] = x_ref[...] + y


def _out_proj(x, m_mla, m_hg, m_rw, w_out, post_norm, tm=256):
    n = x.shape[0]
    return pl.pallas_call(
        _out_proj_kernel,
        grid=(n // tm,),
        in_specs=[pl.BlockSpec((tm, 2048), lambda i: (i, 0)),
                  pl.BlockSpec((tm, 2048), lambda i: (i, 0)),
                  pl.BlockSpec((tm, 1024), lambda i: (i, 0)),
                  pl.BlockSpec((tm, 1024), lambda i: (i, 0)),
                  pl.BlockSpec((4096, 2048), lambda i: (0, 0), pipeline_mode=pl.Buffered(1)),
                  pl.BlockSpec((1, 2048), lambda i: (0, 0))],
        out_specs=pl.BlockSpec((tm, 2048), lambda i: (i, 0)),
        out_shape=jax.ShapeDtypeStruct((n, 2048), F32),
        compiler_params=_cparams(("parallel",)),
        name="out_proj",
    )(x, m_mla, m_hg, m_rw, w_out, post_norm.reshape(1, -1))


def _prep_w_in(w_in):
    o = np.cumsum([0, 512, 256, 64, 2048, 1024, 1024, 1024, 1024, 3200, 1024])
    sl = lambda a, b: w_in[:, :, a:b]
    kr = sl(o[2], o[3])
    kr_sw = jnp.concatenate([-kr[..., 32:], kr[..., :32]], axis=-1)
    rw = o[8]
    parts = [sl(o[0], o[1]), sl(o[1], o[2]), kr, sl(rw + 3072, rw + 3136), kr_sw, sl(rw + 3136, rw + 3200),
             sl(o[4], o[5]), sl(o[3], o[4]), sl(o[5], o[6]), sl(o[6], o[7]), sl(o[7], o[8]),
             sl(o[9], o[10]), sl(rw, rw + 1024), sl(rw + 1024, rw + 2048), sl(rw + 2048, rw + 3072)]
    return jnp.concatenate(parts, axis=-1).astype(BF16)


def _prep_wq(w_q_up):
    d, r, h, _ = w_q_up.shape
    nope = w_q_up[..., :MLA_NOPE].reshape(d, r, h * MLA_NOPE)
    pe = w_q_up[..., MLA_NOPE:]
    pe_sw = jnp.concatenate([-pe[..., 32:], pe[..., :32]], axis=-1)
    pad = lambda a: jnp.pad(a, ((0, 0), (0, 0), (0, 0), (0, 64))).reshape(d, r, h * 128)
    return jnp.concatenate([nope, pad(pe), pad(pe_sw)], axis=-1).astype(BF16)


def _reorder_shift(v):
    z = jnp.zeros(v.shape[:-1] + (64,), v.dtype)
    return jnp.concatenate([v[..., :3072], z, v[..., 3072:3136], z, v[..., 3136:3200]], axis=-1)


def kernel(x_prompt, x_sample, cache_mla, page_table, state_hgrn, state_rwkv, state_rwkv_shift, pre_norm, post_norm, w_in, w_out, mla_q_norm, mla_kv_norm, mla_w_q_up, mla_w_kv_up, hg_lower_bound, hg_g_norm, rw_mu, rw_w0, rw_w_up, rw_a0, rw_a_up, rw_k_k, rw_k_a, rw_r_k, rw_gn_w, rw_gn_b):
    nb, t, d = x_prompt.shape
    db, dt, _ = x_sample.shape
    n_pages = page_table.shape[1]
    past_len = n_pages * PAGE
    sp = SAMPLE_PAD

    w_in_r = _prep_w_in(w_in)
    wq_r = _prep_wq(mla_w_q_up)
    wuk = jnp.transpose(mla_w_kv_up[..., :MLA_NOPE], (0, 2, 3, 1)).astype(BF16)
    wuv = jnp.transpose(mla_w_kv_up[..., MLA_NOPE:], (0, 2, 1, 3)).astype(BF16)
    w_out_b = w_out.astype(BF16)
    zpad = jnp.zeros((DEPTH, 64, RW_WIDTH), F32)
    w_up_p = jnp.concatenate([zpad, rw_w_up], axis=1).astype(BF16)
    a_up_p = jnp.concatenate([zpad, rw_a_up], axis=1).astype(BF16)
    mu_r = _reorder_shift(rw_mu)
    sh0_r = _reorder_shift(state_rwkv_shift).reshape(DEPTH, db, 1, RW_SHIFT_PAD)
    half = MLA_ROPE // 2
    inv = ROPE_BASE ** (-jnp.arange(half, dtype=F32) / half)
    inv128 = jnp.tile(inv, 4).reshape(1, 128)
    pos_p = jnp.tile(jnp.arange(t, dtype=F32), nb).reshape(nb * t, 1)
    pos_s = jnp.tile(past_len + jnp.arange(sp, dtype=F32), db).reshape(db * sp, 1)
    pt_flat = page_table.reshape(-1)

    xp = x_prompt.reshape(nb * t, d)
    xs = jnp.pad(x_sample, ((0, 0), (0, sp - dt), (0, 0))).reshape(db * sp, d)

    rows_p, rows_s, hg_p, hg_s, rw_p, rw_s, sh_p, sh_s = [], [], [], [], [], [], [], []
    for l in range(DEPTH):
        prm = {"mu": mu_r[l:l + 1], "w0": rw_w0[l:l + 1], "w_up": w_up_p[l], "a0": rw_a0[l:l + 1],
               "a_up": a_up_p[l], "k_k": rw_k_k[l:l + 1], "k_a": rw_k_a[l:l + 1],
               "r_k": rw_r_k[l].reshape(1, RW_WIDTH), "gn_w": rw_gn_w[l:l + 1], "gn_b": rw_gn_b[l:l + 1]}

        proj = _matmul(_rms_cast(xp, pre_norm[l]), w_in_r[l])
        qc, rows, kc = _mla_prep(proj, pos_p, inv128, mla_q_norm[l], mla_kv_norm[l], wq_r[l], wuk[l])
        m_mla = _attn_prompt(qc, kc, proj, wuv[l], nb, t)
        m_hg, hg_state = _hgrn(proj, hg_lower_bound, hg_g_norm[l], None, l, nb, t, 128, 128)
        m_rw, rw_state = _rwkv(proj, prm, None, None, nb, t, 64, 64)
        rw_off = COL_RWR * 1024
        sh = jnp.concatenate([proj.reshape(nb, t, D_IN_PAD)[:, t - 1, rw_off:rw_off + 3072],
                              proj.reshape(nb, t, D_IN_PAD)[:, t - 1, 832:896],
                              proj.reshape(nb, t, D_IN_PAD)[:, t - 1, 960:1024]], axis=-1)
        xp = _out_proj(xp, m_mla, m_hg, m_rw, w_out_b[l], post_norm[l])
        rows_p.append(rows.reshape(nb, t, MLA_CACHE)); hg_p.append(hg_state); rw_p.append(rw_state); sh_p.append(sh)

        proj = _matmul(_rms_cast(xs, pre_norm[l]), w_in_r[l])
        qc, rows, kc = _mla_prep(proj, pos_s, inv128, mla_q_norm[l], mla_kv_norm[l], wq_r[l], wuk[l])
        q_s = qc.reshape(MLA_HEADS, db, sp, MLA_QK)[:, :, :dt]
        q_s = jnp.transpose(q_s, (1, 0, 2, 3)).reshape(db, MLA_HEADS * dt, MLA_QK)
        newk = jnp.pad(kc.reshape(db, sp, MLA_QK)[:, :dt], ((0, 0), (0, 16 - dt), (0, 0)))
        o_lat = _attn_sample(pt_flat, q_s, newk, cache_mla, l, n_pages, dt)
        o_lat = jnp.transpose(o_lat.reshape(db, MLA_HEADS, dt, MLA_KVR), (1, 0, 2, 3))
        o_lat = jnp.pad(o_lat, ((0, 0), (0, 0), (0, sp - dt), (0, 0))).reshape(MLA_HEADS, db * sp, MLA_KVR)
        m_mla = _oproj_gate(o_lat, proj, wuv[l])
        m_hg, hg_state = _hgrn(proj, hg_lower_bound, hg_g_norm[l], state_hgrn[l], l, db, sp, sp, dt)
        m_rw, rw_state = _rwkv(proj, prm, sh0_r[l], state_rwkv[l], db, sp, sp, dt)
        pr = proj.reshape(db, sp, D_IN_PAD)[:, dt - 1]
        sh = jnp.concatenate([pr[:, rw_off:rw_off + 3072], pr[:, 832:896], pr[:, 960:1024]], axis=-1)
        xs = _out_proj(xs, m_mla, m_hg, m_rw, w_out_b[l], post_norm[l])
        rows_s.append(rows.reshape(db, sp, MLA_CACHE)[:, :dt]); hg_s.append(hg_state); rw_s.append(rw_state); sh_s.append(sh)

    y_p = xp.reshape(nb, t, d)
    y_s = xs.reshape(db, sp, d)[:, :dt]
    return (y_p, y_s, jnp.stack(rows_p), jnp.stack(rows_s), jnp.stack(hg_p), jnp.stack(hg_s),
            jnp.stack(rw_p), jnp.stack(rw_s), jnp.stack(sh_p), jnp.stack(sh_s))
```

```python
import functools
import math

import jax
import jax.numpy as jnp
import numpy as np
from jax import lax
from jax.experimental import pallas as pl
from jax.experimental.pallas import tpu as pltpu

F32 = jnp.float32
BF16 = jnp.bfloat16

D_MODEL = 2048
DEPTH = 4
PAGE = 128
MLA_HEADS = 16
MLA_NOPE = 128
MLA_ROPE = 64
MLA_V = 128
MLA_QR = 512
MLA_KVR = 256
MLA_CACHE = MLA_KVR + MLA_ROPE
MLA_QK = MLA_KVR + 128
MLA_SCALE = 1.0 / math.sqrt(MLA_NOPE + MLA_ROPE)
ROPE_BASE = 10000.0
NEG_BIG = -1e30
HG_HEADS = 8
HG_D = 128
HG_F_MIN = 1e-30
RW_HEADS = 16
RW_HEAD = 64
RW_WIDTH = 1024
RW_LORA = 64
RW_SHIFT = 3 * RW_WIDTH + 2 * RW_LORA
RW_SHIFT_PAD = 3 * RW_WIDTH + 256
GN_EPS = 64e-5
NORM_EPS = 1e-6
SAMPLE_PAD = 8
VMEM_LIMIT = 48 * 1024 * 1024

COL_SMALL, COL_HGQ, COL_GMLA, COL_HGF, COL_HGI, COL_GHG, COL_GRW, COL_RWR, COL_RWK, COL_RWV = (
    0, 1, 2, 4, 5, 6, 7, 8, 9, 10)
D_IN_PAD = 11 * 1024


def _cparams(sem):
    return pltpu.CompilerParams(dimension_semantics=sem, vmem_limit_bytes=VMEM_LIMIT)


def _dot(a, b):
    return jnp.dot(a.astype(BF16), b.astype(BF16), preferred_element_type=F32)


def _dot_nt(a, b):
    return lax.dot_general(a.astype(BF16), b.astype(BF16), (((1,), (1,)), ((), ())),
                           preferred_element_type=F32)


def _dot_tn(a, b):
    return lax.dot_general(a.astype(BF16), b.astype(BF16), (((0,), (0,)), ((), ())),
                           preferred_element_type=F32)


def _dot_split(m01, x):
    m = m01.astype(BF16)
    hi = x.astype(BF16)
    r1 = x - hi.astype(F32)
    mid = r1.astype(BF16)
    lo = (r1 - mid.astype(F32)).astype(BF16)
    d = lambda p: jnp.dot(m, p, preferred_element_type=F32)
    return d(hi) + d(mid) + d(lo)


def _dot_split_r(x, m01):
    m = m01.astype(BF16)
    hi = x.astype(BF16)
    r1 = x - hi.astype(F32)
    mid = r1.astype(BF16)
    lo = (r1 - mid.astype(F32)).astype(BF16)
    d = lambda p: jnp.dot(p, m, preferred_element_type=F32)
    return d(hi) + d(mid) + d(lo)


def _silu(x):
    return x * jax.nn.sigmoid(x)


def _iota(shape, dim):
    return lax.broadcasted_iota(jnp.int32, shape, dim)


def _rms_cast_kernel(x_ref, g_ref, o_ref):
    x = x_ref[...]
    y = x * lax.rsqrt(jnp.mean(x * x, axis=-1, keepdims=True) + NORM_EPS)
    o_ref[...] = (y * g_ref[...]).astype(BF16)


def _rms_cast(x, g, tm=512):
    n, d = x.shape
    return pl.pallas_call(
        _rms_cast_kernel,
        grid=(n // tm,),
        in_specs=[pl.BlockSpec((tm, d), lambda i: (i, 0)),
                  pl.BlockSpec((1, d), lambda i: (0, 0))],
        out_specs=pl.BlockSpec((tm, d), lambda i: (i, 0)),
        out_shape=jax.ShapeDtypeStruct((n, d), BF16),
        compiler_params=_cparams(("parallel",)),
        name="rms_cast",
    )(x, g.reshape(1, d))


def _matmul_kernel(a_ref, b_ref, o_ref):
    o_ref[...] = jnp.dot(a_ref[...], b_ref[...], preferred_element_type=F32)


def _matmul(a, b, tm=512, tn=1024):
    m, k = a.shape
    _, n = b.shape
    return pl.pallas_call(
        _matmul_kernel,
        grid=(n // tn, m // tm),
        in_specs=[pl.BlockSpec((tm, k), lambda j, i: (i, 0)),
                  pl.BlockSpec((k, tn), lambda j, i: (0, j))],
        out_specs=pl.BlockSpec((tm, tn), lambda j, i: (i, j)),
        out_shape=jax.ShapeDtypeStruct((m, n), F32),
        compiler_params=_cparams(("parallel", "parallel")),
        name="in_proj",
    )(a, b)


def _mla_prep_kernel(p_ref, pos_ref, inv_ref, qn_ref, kvn_ref, wq_ref, wuk_ref,
                     qc_ref, rows_ref, kc_ref):
    x = p_ref[...]
    qd = x[:, 0:MLA_QR]
    kvd = x[:, MLA_QR:MLA_QR + MLA_KVR]
    kr = x[:, 768:832]
    krsw = x[:, 896:960]
    qn = qd * lax.rsqrt(jnp.mean(qd * qd, axis=-1, keepdims=True) + NORM_EPS) * qn_ref[...]
    q = _dot(qn, wq_ref[...])
    ang = pos_ref[...] * inv_ref[...]
    cos = jnp.cos(ang)
    sin = jnp.sin(ang)
    hw = MLA_HEADS * 128
    for h in range(MLA_HEADS):
        ql = _dot(q[:, 128 * h:128 * (h + 1)], wuk_ref[h])
        pe = (q[:, hw + 128 * h:hw + 128 * (h + 1)] * cos
              + q[:, 2 * hw + 128 * h:2 * hw + 128 * (h + 1)] * sin)
        qc_ref[h, :, 0:MLA_KVR] = (ql * MLA_SCALE).astype(BF16)
        qc_ref[h, :, MLA_KVR:MLA_QK] = (pe * MLA_SCALE).astype(BF16)
    ckv = kvd * lax.rsqrt(jnp.mean(kvd * kvd, axis=-1, keepdims=True) + NORM_EPS) * kvn_ref[...]
    kpe = kr * cos[:, 0:64] + krsw * sin[:, 0:64]
    rows_ref[:, 0:MLA_KVR] = ckv
    rows_ref[:, MLA_KVR:MLA_CACHE] = kpe
    kc_ref[:, 0:MLA_KVR] = ckv.astype(BF16)
    kc_ref[:, MLA_KVR:MLA_CACHE] = kpe.astype(BF16)
    kc_ref[:, MLA_CACHE:MLA_QK] = jnp.zeros((x.shape[0], MLA_QK - MLA_CACHE), BF16)


def _mla_prep(proj, pos, inv128, q_norm, kv_norm, wq, wuk, tq=256):
    n = proj.shape[0]
    return pl.pallas_call(
        _mla_prep_kernel,
        grid=(n // tq,),
        in_specs=[pl.BlockSpec((tq, 1024), lambda i: (i, COL_SMALL)),
                  pl.BlockSpec((tq, 1), lambda i: (i, 0)),
                  pl.BlockSpec((1, 128), lambda i: (0, 0)),
                  pl.BlockSpec((1, MLA_QR), lambda i: (0, 0)),
                  pl.BlockSpec((1, MLA_KVR), lambda i: (0, 0)),
                  pl.BlockSpec(wq.shape, lambda i: (0, 0)),
                  pl.BlockSpec(wuk.shape, lambda i: (0, 0, 0))],
        out_specs=[pl.BlockSpec((MLA_HEADS, tq, MLA_QK), lambda i: (0, i, 0)),
                   pl.BlockSpec((tq, MLA_CACHE), lambda i: (i, 0)),
                   pl.BlockSpec((tq, MLA_QK), lambda i: (i, 0))],
        out_shape=[jax.ShapeDtypeStruct((MLA_HEADS, n, MLA_QK), BF16),
                   jax.ShapeDtypeStruct((n, MLA_CACHE), F32),
                   jax.ShapeDtypeStruct((n, MLA_QK), BF16)],
        compiler_params=_cparams(("parallel",)),
        name="mla_prep",
    )(proj, pos, inv128, q_norm.reshape(1, -1), kv_norm.reshape(1, -1), wq, wuk)


def _attn_prompt_kernel(q_ref, k_ref, g_ref, wuv_ref, o_ref, m_sc, l_sc, acc_sc, *, tq, tk, hg):
    i = pl.program_id(1)
    j = pl.program_id(2)
    last = (i * tq + tq - 1) // tk
    rows = MLA_HEADS * tq

    @pl.when(j == 0)
    def _():
        m_sc[...] = jnp.full(m_sc.shape, NEG_BIG, F32)
        l_sc[...] = jnp.zeros(l_sc.shape, F32)
        acc_sc[...] = jnp.zeros(acc_sc.shape, F32)

    def step(masked):
        k = k_ref[...]
        grows = hg * tq
        for g in range(MLA_HEADS // hg):
            rs = slice(g * grows, (g + 1) * grows)
            q = q_ref[g * hg:(g + 1) * hg].reshape(grows, MLA_QK)
            s = lax.dot_general(q, k, (((1,), (1,)), ((), ())), preferred_element_type=F32)
            if masked:
                qpos = i * tq + (_iota((grows, 1), 0) & (tq - 1))
                kpos = j * tk + _iota((1, tk), 1)
                s = jnp.where(kpos <= qpos, s, NEG_BIG)
            m_prev = m_sc[rs, :]
            m_new = jnp.maximum(m_prev, jnp.max(s, axis=-1, keepdims=True))
            alpha = jnp.exp(m_prev - m_new)
            p = jnp.exp(s - m_new)
            l_sc[rs, :] = alpha * l_sc[rs, :] + jnp.sum(p, axis=-1, keepdims=True)
            acc_sc[rs, :] = alpha * acc_sc[rs, :] + jnp.dot(p.astype(BF16), k[:, 0:MLA_KVR],
                                                            preferred_element_type=F32)
            m_sc[rs, :] = m_new

    @pl.when(j < last)
    def _():
        step(False)

    @pl.when(j == last)
    def _():
        step(True)

    @pl.when(j == last)
    def _():
        o_lat = acc_sc[...] / l_sc[...]
        g = g_ref[...]
        for h in range(MLA_HEADS):
            oh = _dot(o_lat[h * tq:(h + 1) * tq, :], wuv_ref[h])
            o_ref[:, 128 * h:128 * (h + 1)] = (oh * _silu(g[:, 128 * h:128 * (h + 1)])).astype(BF16)


def _attn_prompt(qc, kc, proj, wuv, nb, t, tq=128, tk=512, hg=2):
    nq = t // tq
    nk = t // tk
    assert tk % tq == 0 and MLA_HEADS % hg == 0
    kern = functools.partial(_attn_prompt_kernel, tq=tq, tk=tk, hg=hg)

    def kmap(b, i, j):
        return (b * nk + jnp.minimum(j, (i * tq + tq - 1) // tk), 0)

    return pl.pallas_call(
        kern,
        grid=(nb, nq, nk),
        in_specs=[pl.BlockSpec((MLA_HEADS, tq, MLA_QK), lambda b, i, j: (0, b * nq + i, 0)),
                  pl.BlockSpec((tk, MLA_QK), kmap),
                  pl.BlockSpec((tq, 2048), lambda b, i, j: (b * nq + i, COL_GMLA // 2)),
                  pl.BlockSpec(wuv.shape, lambda b, i, j: (0, 0, 0))],
        out_specs=pl.BlockSpec((tq, 2048), lambda b, i, j: (b * nq + i, 0)),
        out_shape=jax.ShapeDtypeStruct((nb * t, 2048), BF16),
        scratch_shapes=[pltpu.VMEM((MLA_HEADS * tq, 1), F32),
                        pltpu.VMEM((MLA_HEADS * tq, 1), F32),
                        pltpu.VMEM((MLA_HEADS * tq, MLA_KVR), F32)],
        compiler_params=_cparams(("parallel", "parallel", "arbitrary")),
        name="attn_prompt",
    )(qc, kc, proj, wuv)


def _attn_sample_kernel(pt_ref, q_ref, nk_ref, *rest, layer, nseq, npg, dt):
    pages = rest[:nseq * npg]
    o_ref = rest[nseq * npg]
    m_sc, l_sc, acc_sc = rest[nseq * npg + 1:]
    j = pl.program_id(1)
    rows = MLA_HEADS * dt

    @pl.when(j == 0)
    def _():
        for s in range(nseq):
            q = q_ref[s]
            kn = nk_ref[s]
            sc = lax.dot_general(q, kn, (((1,), (1,)), ((), ())), preferred_element_type=F32)
            trow = _iota((rows, 1), 0) & (dt - 1)
            kcol = _iota((1, kn.shape[0]), 1)
            sc = jnp.where(kcol <= trow, sc, NEG_BIG)
            m = jnp.max(sc, axis=-1, keepdims=True)
            p = jnp.exp(sc - m)
            m_sc[s] = m
            l_sc[s] = jnp.sum(p, axis=-1, keepdims=True)
            acc_sc[s] = jnp.dot(p.astype(BF16), kn[:, 0:MLA_KVR], preferred_element_type=F32)

    seqs = range(nseq)
    kts = [jnp.concatenate([pages[s * npg + g][...].astype(BF16) for g in range(npg)], axis=1) for s in seqs]
    scs = [jnp.dot(q_ref[s][:, 0:MLA_CACHE], kts[s], preferred_element_type=F32) for s in seqs]
    m_prev = [m_sc[s] for s in seqs]
    l_prev = [l_sc[s] for s in seqs]
    a_prev = [acc_sc[s] for s in seqs]
    m_new = [jnp.maximum(m_prev[s], jnp.max(scs[s], axis=-1, keepdims=True)) for s in seqs]
    alpha = [jnp.exp(m_prev[s] - m_new[s]) for s in seqs]
    ps = [jnp.exp(scs[s] - m_new[s]) for s in seqs]
    pv = [lax.dot_general(ps[s].astype(BF16), kts[s][0:MLA_KVR, :], (((1,), (1,)), ((), ())),
                          preferred_element_type=F32) for s in seqs]
    for s in seqs:
        l_sc[s] = alpha[s] * l_prev[s] + jnp.sum(ps[s], axis=-1, keepdims=True)
        acc_sc[s] = alpha[s] * a_prev[s] + pv[s]
        m_sc[s] = m_new[s]

    @pl.when(j == pl.num_programs(1) - 1)
    def _():
        for s in range(nseq):
            o_ref[s] = acc_sc[s] / l_sc[s]


def _attn_sample(page_table_flat, q_s, newk, cache, layer, n_pages, dt, nseq=4, npg=8):
    db = q_s.shape[0]
    rows = MLA_HEADS * dt
    ngrp = n_pages // npg
    kern = functools.partial(_attn_sample_kernel, layer=layer, nseq=nseq, npg=npg, dt=dt)

    def page_spec(s, g):
        def imap(bb, j, pt):
            return (layer, pt[(bb * nseq + s) * n_pages + j * npg + g], 0, 0)
        return pl.BlockSpec((None, None, MLA_CACHE, PAGE), imap)

    in_specs = [pl.BlockSpec((nseq, rows, MLA_QK), lambda bb, j, pt: (bb, 0, 0)),
                pl.BlockSpec((nseq, 16, MLA_QK), lambda bb, j, pt: (bb, 0, 0))]
    in_specs += [page_spec(s, g) for s in range(nseq) for g in range(npg)]
    grid_spec = pltpu.PrefetchScalarGridSpec(
        num_scalar_prefetch=1,
        grid=(db // nseq, ngrp),
        in_specs=in_specs,
        out_specs=pl.BlockSpec((nseq, rows, MLA_KVR), lambda bb, j, pt: (bb, 0, 0)),
        scratch_shapes=[pltpu.VMEM((nseq, rows, 1), F32),
                        pltpu.VMEM((nseq, rows, 1), F32),
                        pltpu.VMEM((nseq, rows, MLA_KVR), F32)],
    )
    return pl.pallas_call(
        kern,
        grid_spec=grid_spec,
        out_shape=jax.ShapeDtypeStruct((db, rows, MLA_KVR), F32),
        compiler_params=_cparams(("parallel", "arbitrary")),
        name="attn_sample",
    )(page_table_flat, q_s, newk, *([cache] * (nseq * npg)))


def _oproj_gate_kernel(o_ref, g_ref, wuv_ref, out_ref):
    g = g_ref[...]
    for h in range(MLA_HEADS):
        oh = _dot(o_ref[h], wuv_ref[h])
        out_ref[:, 128 * h:128 * (h + 1)] = (oh * _silu(g[:, 128 * h:128 * (h + 1)])).astype(BF16)


def _oproj_gate(o_lat, proj, wuv, tm=256):
    n = o_lat.shape[1]
    return pl.pallas_call(
        _oproj_gate_kernel,
        grid=(n // tm,),
        in_specs=[pl.BlockSpec((MLA_HEADS, tm, MLA_KVR), lambda i: (0, i, 0)),
                  pl.BlockSpec((tm, 2048), lambda i: (i, COL_GMLA // 2)),
                  pl.BlockSpec(wuv.shape, lambda i: (0, 0, 0))],
        out_specs=pl.BlockSpec((tm, 2048), lambda i: (i, 0)),
        out_shape=jax.ShapeDtypeStruct((n, 2048), BF16),
        compiler_params=_cparams(("parallel",)),
        name="oproj_gate",
    )(o_lat, proj, wuv)


def _hgrn_kernel(lb_ref, gn_ref, q_ref, f_ref, i_ref, g_ref, *rest, layer, c, valid, has_init):
    if has_init:
        s0_ref, o_ref, sout_ref, st_sc = rest
    else:
        o_ref, sout_ref, st_sc = rest
    ci = pl.program_id(1)

    @pl.when(ci == 0)
    def _():
        if has_init:
            for h in range(HG_HEADS):
                st_sc[h] = s0_ref[0, h].T
        else:
            st_sc[...] = jnp.zeros(st_sc.shape, F32)

    lbx = lb_ref[...]
    e = jnp.exp(lbx - jnp.max(lbx, axis=0, keepdims=True))
    sm = e / jnp.sum(e, axis=0, keepdims=True)
    lb = jnp.zeros((1, lbx.shape[1]), F32)
    for l in range(1, layer + 1):
        lb = lb + sm[l:l + 1, :]

    z = f_ref[...]
    q_all = _silu(q_ref[...])
    f = lb + (1.0 - lb) * jax.nn.sigmoid(z)
    logf_all = jnp.log(jnp.maximum(f, HG_F_MIN))
    k_all = (1.0 - lb) * jax.nn.sigmoid(-z)
    v_all = i_ref[...]
    t_col = _iota((c, 1), 0)
    if valid < c:
        live = t_col < valid
        logf_all = jnp.where(live, logf_all, 0.0)
        k_all = jnp.where(live, k_all, 0.0)

    t_r = _iota((c, c), 0)
    t_c = _iota((c, c), 1)
    tri = (t_c <= t_r).astype(F32)
    m4 = t_r & 3
    lvl2 = (((m4 == 2) & (t_c == t_r)) | ((m4 == 3) & ((t_c == t_r) | (t_c == t_r - 1)))
            | ((m4 == 0) & (t_c == t_r + 1))).astype(F32)
    cum_all = _dot_split(tri, logf_all)
    e2_all = _dot_split(lvl2, logf_all) if c >= 4 else None

    for h in range(HG_HEADS):
        sl = slice(128 * h, 128 * (h + 1))
        q, k, v, logf, cum = q_all[:, sl], k_all[:, sl], v_all[:, sl], logf_all[:, sl], cum_all[:, sl]
        a = jnp.zeros((c, c), F32)
        half = c // 2
        while half >= 1:
            upper = (t_col & (2 * half - 1)) >= half
            if half >= 4:
                nblk = c // (2 * half)
                c3 = cum.reshape(nblk, 2 * half, 128)
                ref = jnp.broadcast_to(c3[:, half - 1:half, :], (nblk, 2 * half, 128)).reshape(c, 128)
                eq = jnp.where(upper, cum - ref, NEG_BIG)
                ek = jnp.where(upper, NEG_BIG, ref - cum)
            elif half == 2:
                e2 = e2_all[:, sl]
                eq = jnp.where(upper, e2, NEG_BIG)
                ek = jnp.where(upper, NEG_BIG, e2)
            else:
                eq = jnp.where(upper, logf, NEG_BIG)
                ek = jnp.where(upper, NEG_BIG, 0.0)
            prod = _dot_nt(q * jnp.exp(eq), k * jnp.exp(ek))
            shift = int(math.log2(2 * half))
            a = a + jnp.where((t_r >> shift) == (t_c >> shift), prod, 0.0)
            half //= 2
        a = jnp.where(t_r == t_c, jnp.sum(q * k, axis=-1, keepdims=True), a)
        st = st_sc[h]
        o = _dot(a, v) + _dot_nt(q * jnp.exp(cum), st)
        last = cum[c - 1:c, :]
        st_sc[h] = st * jnp.exp(last) + _dot_tn(v, k * jnp.exp(last - cum))
        o = o * lax.rsqrt(jnp.mean(o * o, axis=-1, keepdims=True) + NORM_EPS) * gn_ref[...]
        o_ref[:, sl] = (o * _silu(g_ref[:, sl])).astype(o_ref.dtype)

    @pl.when(ci == pl.num_programs(1) - 1)
    def _():
        for h in range(HG_HEADS):
            sout_ref[0, h] = st_sc[h].T


def _hgrn(proj, lb_raw, g_norm, s0, layer, nseq, t, c, valid):
    nc = t // c
    has_init = s0 is not None
    kern = functools.partial(_hgrn_kernel, layer=layer, c=c, valid=valid, has_init=has_init)

    def col(cb):
        return pl.BlockSpec((c, 1024), lambda b, ci: (b * nc + ci, cb))

    in_specs = [pl.BlockSpec((DEPTH, 1024), lambda b, ci: (0, 0)),
                pl.BlockSpec((1, HG_D), lambda b, ci: (0, 0)),
                col(COL_HGQ), col(COL_HGF), col(COL_HGI), col(COL_GHG)]
    args = [lb_raw, g_norm.reshape(1, HG_D), proj, proj, proj, proj]
    if has_init:
        in_specs.append(pl.BlockSpec((1, HG_HEADS, HG_D, HG_D), lambda b, ci: (b, 0, 0, 0)))
        args.append(s0)
    return pl.pallas_call(
        kern,
        grid=(nseq, nc),
        in_specs=in_specs,
        out_specs=[pl.BlockSpec((c, 1024), lambda b, ci: (b * nc + ci, 0)),
                   pl.BlockSpec((1, HG_HEADS, HG_D, HG_D), lambda b, ci: (b, 0, 0, 0))],
        out_shape=[jax.ShapeDtypeStruct((nseq * t, 1024), BF16 if c % 16 == 0 else F32),
                   jax.ShapeDtypeStruct((nseq, HG_HEADS, HG_D, HG_D), F32)],
        scratch_shapes=[pltpu.VMEM((HG_HEADS, HG_D, HG_D), F32)],
        compiler_params=_cparams(("parallel", "arbitrary")),
        name="hgrn",
    )(*args)


def _rwkv_kernel(xr_ref, xk_ref, xv_ref, xs_ref, g_ref, mu_ref, w0_ref, wup_ref, a0_ref, aup_ref,
                 kk_ref, ka_ref, rk_ref, gnw_ref, gnb_ref, *rest, c, nsb, valid, has_init):
    if has_init:
        sh0_ref, s0_ref, o_ref, sout_ref = rest
    else:
        o_ref, sout_ref, s_sc, carry_sc = rest
        ci = pl.program_id(1)
    npair = RW_HEADS // 2
    rows = nsb * c
    lc = int(math.log2(c))
    w = RW_WIDTH

    if not has_init:
        @pl.when(ci == 0)
        def _():
            carry_sc[...] = jnp.zeros(carry_sc.shape, F32)
            s_sc[...] = jnp.zeros(s_sc.shape, F32)

    t_col = _iota((rows, 1), 0)
    first = (t_col & (c - 1)) == 0
    if has_init:
        sh0 = jnp.broadcast_to(sh0_ref[...], (nsb, c, RW_SHIFT_PAD)).reshape(rows, RW_SHIFT_PAD)

    def shifted(x, lo, hi):
        before = sh0[:, lo:hi] if has_init else carry_sc[0:1, lo:hi]
        prev = jnp.where(first, before, pltpu.roll(x, 1, 0))
        return x + mu_ref[:, lo:hi] * (prev - x)

    xr, xk, xv = xr_ref[...], xk_ref[...], xv_ref[...]
    xs = xs_ref[...]
    xw = xs[:, 768:896]
    xa = xs[:, 896:1024]
    r = shifted(xr, 0, w)
    k = shifted(xk, w, 2 * w)
    v = shifted(xv, 2 * w, 3 * w)
    wl = shifted(xw, 3 * w, 3 * w + 128)
    al = shifted(xa, 3 * w + 128, 3 * w + 256)
    if not has_init:
        carry_sc[0:1, 0:w] = xr[c - 1:c, :]
        carry_sc[0:1, w:2 * w] = xk[c - 1:c, :]
        carry_sc[0:1, 2 * w:3 * w] = xv[c - 1:c, :]
        carry_sc[0:1, 3 * w:3 * w + 128] = xw[c - 1:c, :]
        carry_sc[0:1, 3 * w + 128:3 * w + 256] = xa[c - 1:c, :]

    zw = w0_ref[...] + _dot(jnp.tanh(wl), wup_ref[...])
    log_w = -math.exp(-0.5) * jax.nn.sigmoid(zw)
    ag = jax.nn.sigmoid(a0_ref[...] + _dot(al, aup_ref[...]))

    seg_b = (_iota((128, 128), 0) >> 6) == (_iota((128, 128), 1) >> 6)
    seg = seg_b.astype(BF16)

    def seg_sum(x):
        return jnp.concatenate(
            [jnp.dot(x[:, 128 * p:128 * (p + 1)].astype(BF16), seg, preferred_element_type=F32)
             for p in range(npair)], axis=1)

    kk = k * kk_ref[...]
    kk = kk * lax.rsqrt(jnp.maximum(seg_sum(kk * kk), 1e-24))
    kf = k * (1.0 + (ag - 1.0) * ka_ref[...])
    if valid < c:
        live = (t_col & (c - 1)) < valid
        kk = jnp.where(live, kk, 0.0)
        kf = jnp.where(live, kf, 0.0)
        log_w = jnp.where(live, log_w, 0.0)

    t_r = _iota((rows, rows), 0)
    t_c = _iota((rows, rows), 1)
    tri = ((t_c <= t_r) & ((t_r >> lc) == (t_c >> lc))).astype(F32)
    eye = (t_c == t_r).astype(F32)
    cum = _dot_split(tri, log_w)
    e_in = jnp.exp(cum)
    e_out = jnp.exp(-cum)
    at_all = -kk * jnp.exp(cum - log_w)
    bt_all = kk * ag * e_out
    kt_all = kf * e_out
    rt_all = r * e_in
    r2 = 2 * rows
    a_r = _iota((r2, r2), 0)
    a_c = _iota((r2, r2), 1)
    tr = a_r & (rows - 1)
    tc = a_c & (rows - 1)
    m4 = ((tr >> lc) == (tc >> lc)) & ((tc < tr) | ((a_r >= rows) & (tc == tr)))
    m4_right = m4 & (a_c >= rows)
    head_lo = _iota((1, 128), 1) < RW_HEAD
    nsteps = 0
    cover = 2
    while cover < valid:
        cover *= 2
        nsteps += 1

    def pair(x, p):
        return x[:, 128 * p:128 * (p + 1)]

    if has_init:
        z64 = jnp.zeros((RW_HEAD, RW_HEAD), F32)
        states = [[jnp.concatenate([jnp.concatenate([s0_ref[s, 2 * p], z64], axis=1),
                                    jnp.concatenate([z64, s0_ref[s, 2 * p + 1]], axis=1)], axis=0)
                   for p in range(npair)] for s in range(nsb)]
    else:
        states = [[s_sc[p] for p in range(npair)]]

    ar_at, ar_rt = [], []
    for p in range(npair):
        at_p, rt_p = pair(at_all, p), pair(rt_all, p)
        pa, pr = [], []
        for s in range(nsb):
            rs = slice(s * c, (s + 1) * c)
            ar = _dot_nt(jnp.concatenate([at_p[rs], rt_p[rs]], axis=0), states[s][p])
            pa.append(ar[0:c])
            pr.append(ar[c:2 * c])
        ar_at.append(jnp.concatenate(pa, axis=0) if nsb > 1 else pa[0])
        ar_rt.append(jnp.concatenate(pr, axis=0) if nsb > 1 else pr[0])

    heads = [(p, hh) for p in range(npair) for hh in range(2)]
    hmask = [head_lo if hh == 0 else jnp.logical_not(head_lo) for _, hh in heads]
    labs, mrbs, rhss, ykv = [], [], [], []
    for (p, hh), hm in zip(heads, hmask):
        lhs = jnp.concatenate([jnp.where(hm, pair(at_all, p), 0.0), jnp.where(hm, pair(rt_all, p), 0.0)], axis=0)
        rhs_t = jnp.concatenate([pair(bt_all, p), pair(kt_all, p)], axis=0)
        prod = _dot_nt(lhs, rhs_t)
        vp = pair(v, p)
        lkv = _dot(jnp.where(m4_right, prod, 0.0), jnp.concatenate([vp, vp], axis=0))
        m4p = jnp.where(m4, prod, 0.0)
        labs.append(m4p[0:rows, 0:rows])
        mrbs.append(m4p[rows:r2, 0:rows])
        rhss.append(jnp.where(hm, ar_at[p] + lkv[0:rows], 0.0))
        ykv.append(jnp.where(hm, lkv[rows:r2], 0.0))

    xs_ = [eye + lab for lab in labs]
    pws = labs
    for _ in range(nsteps):
        pws = [_dot(pw, pw) for pw in pws]
        xs_ = [x + _dot(x, pw) for x, pw in zip(xs_, pws)]

    us = [_dot(x, rhs) for x, rhs in zip(xs_, rhss)]
    ys = [_dot(mrb, u) + yk for mrb, u, yk in zip(mrbs, us, ykv)]
    u_pair = [us[2 * p] + us[2 * p + 1] for p in range(npair)]
    y = jnp.concatenate([ar_rt[p] + ys[2 * p] + ys[2 * p + 1] for p in range(npair)], axis=1)

    for p in range(npair):
        bt_p, kt_p, vp, up = pair(bt_all, p), pair(kt_all, p), pair(v, p), u_pair[p]
        for s in range(nsb):
            rs = slice(s * c, (s + 1) * c)
            pc = pair(e_in, p)[s * c + c - 1:s * c + c, :]
            s_new = states[s][p] * pc + _dot_tn(jnp.concatenate([up[rs], vp[rs]], axis=0),
                                                jnp.concatenate([bt_p[rs] * pc, kt_p[rs] * pc], axis=0))
            s_new = jnp.where(seg_b, s_new, 0.0)
            if has_init:
                sout_ref[s, 2 * p] = s_new[0:RW_HEAD, 0:RW_HEAD]
                sout_ref[s, 2 * p + 1] = s_new[RW_HEAD:128, RW_HEAD:128]
            else:
                s_sc[p] = s_new

    mean = seg_sum(y) * (1.0 / RW_HEAD)
    d = y - mean
    var = seg_sum(d * d) * (1.0 / RW_HEAD)
    yn = d * lax.rsqrt(var + GN_EPS) * gnw_ref[...] + gnb_ref[...]
    yn = yn + seg_sum(r * kf * rk_ref[...]) * v
    o_ref[...] = (yn * _silu(g_ref[...])).astype(o_ref.dtype)

    if not has_init:
        @pl.when(ci == pl.num_programs(1) - 1)
        def _():
            for p in range(npair):
                s_bd = s_sc[p]
                sout_ref[0, 2 * p] = s_bd[0:RW_HEAD, 0:RW_HEAD]
                sout_ref[0, 2 * p + 1] = s_bd[RW_HEAD:128, RW_HEAD:128]


def _rwkv(proj, prm, sh0, s0, nseq, t, c, nsb, valid):
    nc = t // c
    has_init = s0 is not None
    assert (nc == 1) if has_init else (nsb == 1)
    rows = nsb * c
    kern = functools.partial(_rwkv_kernel, c=c, nsb=nsb, valid=valid, has_init=has_init)

    def col(cb):
        return pl.BlockSpec((rows, 1024), lambda b, ci: (b * nc + ci, cb))

    def const(shape):
        return pl.BlockSpec(shape, lambda b, ci: (0,) * len(shape))

    in_specs = [col(COL_RWR), col(COL_RWK), col(COL_RWV), col(COL_SMALL), col(COL_GRW),
                const((1, RW_SHIFT_PAD)), const((1, 1024)), const((128, 1024)), const((1, 1024)),
                const((128, 1024)), const((1, 1024)), const((1, 1024)), const((1, 1024)),
                const((1, 1024)), const((1, 1024))]
    args = [proj, proj, proj, proj, proj, prm["mu"], prm["w0"], prm["w_up"], prm["a0"], prm["a_up"],
            prm["k_k"], prm["k_a"], prm["r_k"], prm["gn_w"], prm["gn_b"]]
    scratch = []
    if has_init:
        in_specs += [pl.BlockSpec((nsb, 1, RW_SHIFT_PAD), lambda b, ci: (b, 0, 0)),
                     pl.BlockSpec((nsb, RW_HEADS, RW_HEAD, RW_HEAD), lambda b, ci: (b, 0, 0, 0))]
        args += [sh0, s0]
    else:
        scratch = [pltpu.VMEM((RW_HEADS // 2, 128, 128), F32), pltpu.VMEM((8, RW_SHIFT_PAD), F32)]
    return pl.pallas_call(
        kern,
        grid=(nseq // nsb, nc),
        in_specs=in_specs,
        out_specs=[pl.BlockSpec((rows, 1024), lambda b, ci: (b * nc + ci, 0)),
                   pl.BlockSpec((nsb, RW_HEADS, RW_HEAD, RW_HEAD), lambda b, ci: (b, 0, 0, 0))],
        out_shape=[jax.ShapeDtypeStruct((nseq * t, 1024), BF16),
                   jax.ShapeDtypeStruct((nseq, RW_HEADS, RW_HEAD, RW_HEAD), F32)],
        scratch_shapes=scratch,
        compiler_params=_cparams(("parallel", "arbitrary")),
        name="rwkv",
    )(*args)


def _out_proj_kernel(x_ref, m_ref, hg_ref, rw_ref, w_ref, g_ref, o_ref):
    acc = _dot(m_ref[...], w_ref[0:2048, :])
    acc = acc + _dot(hg_ref[...], w_ref[2048:3072, :])
    acc = acc + _dot(rw_ref[...], w_ref[3072:4096, :])
    y = acc * lax.rsqrt(jnp.mean(acc * acc, axis=-1, keepdims=True) + NORM_EPS) * g_ref[...]
    o_ref[...] = x_ref[...] + y


def _out_proj(x, m_mla, m_hg, m_rw, w_out, post_norm, tm=256):
    n = x.shape[0]
    return pl.pallas_call(
        _out_proj_kernel,
        grid=(n // tm,),
        in_specs=[pl.BlockSpec((tm, 2048), lambda i: (i, 0)),
                  pl.BlockSpec((tm, 2048), lambda i: (i, 0)),
                  pl.BlockSpec((tm, 1024), lambda i: (i, 0)),
                  pl.BlockSpec((tm, 1024), lambda i: (i, 0)),
                  pl.BlockSpec((4096, 2048), lambda i: (0, 0), pipeline_mode=pl.Buffered(1)),
                  pl.BlockSpec((1, 2048), lambda i: (0, 0))],
        out_specs=pl.BlockSpec((tm, 2048), lambda i: (i, 0)),
        out_shape=jax.ShapeDtypeStruct((n, 2048), F32),
        compiler_params=_cparams(("parallel",)),
        name="out_proj",
    )(x, m_mla, m_hg, m_rw, w_out, post_norm.reshape(1, -1))


def _prep_w_in(w_in):
    o = np.cumsum([0, 512, 256, 64, 2048, 1024, 1024, 1024, 1024, 3200, 1024])
    sl = lambda a, b: w_in[:, :, a:b]
    kr = sl(o[2], o[3])
    kr_sw = jnp.concatenate([-kr[..., 32:], kr[..., :32]], axis=-1)
    rw = o[8]
    parts = [sl(o[0], o[1]), sl(o[1], o[2]), kr, sl(rw + 3072, rw + 3136), kr_sw, sl(rw + 3136, rw + 3200),
             sl(o[4], o[5]), sl(o[3], o[4]), sl(o[5], o[6]), sl(o[6], o[7]), sl(o[7], o[8]),
             sl(o[9], o[10]), sl(rw, rw + 1024), sl(rw + 1024, rw + 2048), sl(rw + 2048, rw + 3072)]
    return jnp.concatenate(parts, axis=-1).astype(BF16)


def _prep_wq(w_q_up):
    d, r, h, _ = w_q_up.shape
    nope = w_q_up[..., :MLA_NOPE].reshape(d, r, h * MLA_NOPE)
    pe = w_q_up[..., MLA_NOPE:]
    pe_sw = jnp.concatenate([-pe[..., 32:], pe[..., :32]], axis=-1)
    pad = lambda a: jnp.pad(a, ((0, 0), (0, 0), (0, 0), (0, 64))).reshape(d, r, h * 128)
    return jnp.concatenate([nope, pad(pe), pad(pe_sw)], axis=-1).astype(BF16)


def _reorder_shift(v):
    z = jnp.zeros(v.shape[:-1] + (64,), v.dtype)
    return jnp.concatenate([v[..., :3072], z, v[..., 3072:3136], z, v[..., 3136:3200]], axis=-1)


def kernel(x_prompt, x_sample, cache_mla, page_table, state_hgrn, state_rwkv, state_rwkv_shift, pre_norm, post_norm, w_in, w_out, mla_q_norm, mla_kv_norm, mla_w_q_up, mla_w_kv_up, hg_lower_bound, hg_g_norm, rw_mu, rw_w0, rw_w_up, rw_a0, rw_a_up, rw_k_k, rw_k_a, rw_r_k, rw_gn_w, rw_gn_b):
    nb, t, d = x_prompt.shape
    db, dt, _ = x_sample.shape
    n_pages = page_table.shape[1]
    past_len = n_pages * PAGE
    sp = SAMPLE_PAD

    w_in_r = _prep_w_in(w_in)
    wq_r = _prep_wq(mla_w_q_up)
    wuk = jnp.transpose(mla_w_kv_up[..., :MLA_NOPE], (0, 2, 3, 1)).astype(BF16)
    wuv = jnp.transpose(mla_w_kv_up[..., MLA_NOPE:], (0, 2, 1, 3)).astype(BF16)
    w_out_b = w_out.astype(BF16)
    zpad = jnp.zeros((DEPTH, 64, RW_WIDTH), F32)
    w_up_p = jnp.concatenate([zpad, rw_w_up], axis=1).astype(BF16)
    a_up_p = jnp.concatenate([zpad, rw_a_up], axis=1).astype(BF16)
    mu_r = _reorder_shift(rw_mu)
    sh0_r = _reorder_shift(state_rwkv_shift).reshape(DEPTH, db, 1, RW_SHIFT_PAD)
    half = MLA_ROPE // 2
    inv = ROPE_BASE ** (-jnp.arange(half, dtype=F32) / half)
    inv128 = jnp.tile(inv, 4).reshape(1, 128)
    pos_p = jnp.tile(jnp.arange(t, dtype=F32), nb).reshape(nb * t, 1)
    pos_s = jnp.tile(past_len + jnp.arange(sp, dtype=F32), db).reshape(db * sp, 1)
    pt_flat = page_table.reshape(-1)
    cache_t = jnp.swapaxes(cache_mla, 2, 3)

    xp = x_prompt.reshape(nb * t, d)
    xs = jnp.pad(x_sample, ((0, 0), (0, sp - dt), (0, 0))).reshape(db * sp, d)

    rows_p, rows_s, hg_p, hg_s, rw_p, rw_s, sh_p, sh_s = [], [], [], [], [], [], [], []
    for l in range(DEPTH):
        prm = {"mu": mu_r[l:l + 1], "w0": rw_w0[l:l + 1], "w_up": w_up_p[l], "a0": rw_a0[l:l + 1],
               "a_up": a_up_p[l], "k_k": rw_k_k[l:l + 1], "k_a": rw_k_a[l:l + 1],
               "r_k": rw_r_k[l].reshape(1, RW_WIDTH), "gn_w": rw_gn_w[l:l + 1], "gn_b": rw_gn_b[l:l + 1]}

        proj = _matmul(_rms_cast(xp, pre_norm[l]), w_in_r[l])
        qc, rows, kc = _mla_prep(proj, pos_p, inv128, mla_q_norm[l], mla_kv_norm[l], wq_r[l], wuk[l])
        m_mla = _attn_prompt(qc, kc, proj, wuv[l], nb, t)
        m_hg, hg_state = _hgrn(proj, hg_lower_bound, hg_g_norm[l], None, l, nb, t, 128, 128)
        m_rw, rw_state = _rwkv(proj, prm, None, None, nb, t, 64, 1, 64)
        rw_off = COL_RWR * 1024
        sh = jnp.concatenate([proj.reshape(nb, t, D_IN_PAD)[:, t - 1, rw_off:rw_off + 3072],
                              proj.reshape(nb, t, D_IN_PAD)[:, t - 1, 832:896],
                              proj.reshape(nb, t, D_IN_PAD)[:, t - 1, 960:1024]], axis=-1)
        xp = _out_proj(xp, m_mla, m_hg, m_rw, w_out_b[l], post_norm[l])
        rows_p.append(rows.reshape(nb, t, MLA_CACHE)); hg_p.append(hg_state); rw_p.append(rw_state); sh_p.append(sh)

        proj = _matmul(_rms_cast(xs, pre_norm[l]), w_in_r[l])
        qc, rows, kc = _mla_prep(proj, pos_s, inv128, mla_q_norm[l], mla_kv_norm[l], wq_r[l], wuk[l])
        q_s = qc.reshape(MLA_HEADS, db, sp, MLA_QK)[:, :, :dt]
        q_s = jnp.transpose(q_s, (1, 0, 2, 3)).reshape(db, MLA_HEADS * dt, MLA_QK)
        newk = jnp.pad(kc.reshape(db, sp, MLA_QK)[:, :dt], ((0, 0), (0, 16 - dt), (0, 0)))
        o_lat = _attn_sample(pt_flat, q_s, newk, cache_t, l, n_pages, dt)
        o_lat = jnp.transpose(o_lat.reshape(db, MLA_HEADS, dt, MLA_KVR), (1, 0, 2, 3))
        o_lat = jnp.pad(o_lat, ((0, 0), (0, 0), (0, sp - dt), (0, 0))).reshape(MLA_HEADS, db * sp, MLA_KVR)
        m_mla = _oproj_gate(o_lat, proj, wuv[l])
        m_hg, hg_state = _hgrn(proj, hg_lower_bound, hg_g_norm[l], state_hgrn[l], l, db, sp, sp, dt)
        m_rw, rw_state = _rwkv(proj, prm, sh0_r[l], state_rwkv[l], db, sp, sp, 8, dt)
        pr = proj.reshape(db, sp, D_IN_PAD)[:, dt - 1]
        sh = jnp.concatenate([pr[:, rw_off:rw_off + 3072], pr[:, 832:896], pr[:, 960:1024]], axis=-1)
        xs = _out_proj(xs, m_mla, m_hg, m_rw, w_out_b[l], post_norm[l])
        rows_s.append(rows.reshape(db, sp, MLA_CACHE)[:, :dt]); hg_s.append(hg_state); rw_s.append(rw_state); sh_s.append(sh)

    y_p = xp.reshape(nb, t, d)
    y_s = xs.reshape(db, sp, d)[:, :dt]
    return (y_p, y_s, jnp.stack(rows_p), jnp.stack(rows_s), jnp.stack(hg_p), jnp.stack(hg_s),
            jnp.stack(rw_p), jnp.stack(rw_s), jnp.stack(sh_p), jnp.stack(sh_s))
```

```python
import functools
import math

import jax
import jax.numpy as jnp
import numpy as np
from jax import lax
from jax.experimental import pallas as pl
from jax.experimental.pallas import tpu as pltpu

F32 = jnp.float32
BF16 = jnp.bfloat16

D_MODEL = 2048
DEPTH = 4
PAGE = 128
MLA_HEADS = 16
MLA_NOPE = 128
MLA_ROPE = 64
MLA_V = 128
MLA_QR = 512
MLA_KVR = 256
MLA_CACHE = MLA_KVR + MLA_ROPE
MLA_QK = MLA_KVR + 128
MLA_SCALE = 1.0 / math.sqrt(MLA_NOPE + MLA_ROPE)
ROPE_BASE = 10000.0
NEG_BIG = -1e30
HG_HEADS = 8
HG_D = 128
HG_F_MIN = 1e-30
RW_HEADS = 16
RW_HEAD = 64
RW_WIDTH = 1024
RW_LORA = 64
RW_SHIFT = 3 * RW_WIDTH + 2 * RW_LORA
RW_SHIFT_PAD = 3 * RW_WIDTH + 256
GN_EPS = 64e-5
NORM_EPS = 1e-6
SAMPLE_PAD = 8
VMEM_LIMIT = 48 * 1024 * 1024

COL_GMLA, COL_HGQ, COL_HGF, COL_HGI, COL_GHG, COL_GRW, COL_RWR, COL_RWK, COL_RWV = (0, 2, 3, 4, 5, 6, 7, 8, 9)
MAIN_W_ROWS = (832, 1856, 2880, 3904, 4928, 5952, 10176, 6976, 8000, 9024)
D_MAIN = 10 * 1024
RW_X_ROW = 6976


def _cparams(sem):
    return pltpu.CompilerParams(dimension_semantics=sem, vmem_limit_bytes=VMEM_LIMIT)


def _dot(a, b):
    return jnp.dot(a.astype(BF16), b.astype(BF16), preferred_element_type=F32)


def _dot_nt(a, b):
    return lax.dot_general(a.astype(BF16), b.astype(BF16), (((1,), (1,)), ((), ())),
                           preferred_element_type=F32)


def _dot_tn(a, b):
    return lax.dot_general(a.astype(BF16), b.astype(BF16), (((0,), (0,)), ((), ())),
                           preferred_element_type=F32)


def _dot_split(m01, x):
    m = m01.astype(BF16)
    hi = x.astype(BF16)
    r1 = x - hi.astype(F32)
    mid = r1.astype(BF16)
    lo = (r1 - mid.astype(F32)).astype(BF16)
    d = lambda p: jnp.dot(m, p, preferred_element_type=F32)
    return d(hi) + d(mid) + d(lo)


def _dot_split_r(x, m01):
    m = m01.astype(BF16)
    hi = x.astype(BF16)
    r1 = x - hi.astype(F32)
    mid = r1.astype(BF16)
    lo = (r1 - mid.astype(F32)).astype(BF16)
    d = lambda p: jnp.dot(p, m, preferred_element_type=F32)
    return d(hi) + d(mid) + d(lo)


def _silu(x):
    return x * jax.nn.sigmoid(x)


def _iota(shape, dim):
    return lax.broadcasted_iota(jnp.int32, shape, dim)


def _rms_cast_kernel(x_ref, g_ref, o_ref):
    x = x_ref[...]
    y = x * lax.rsqrt(jnp.mean(x * x, axis=-1, keepdims=True) + NORM_EPS)
    o_ref[...] = (y * g_ref[...]).astype(BF16)


def _rms_cast(x, g, tm=512):
    n, d = x.shape
    return pl.pallas_call(
        _rms_cast_kernel,
        grid=(n // tm,),
        in_specs=[pl.BlockSpec((tm, d), lambda i: (i, 0)),
                  pl.BlockSpec((1, d), lambda i: (0, 0))],
        out_specs=pl.BlockSpec((tm, d), lambda i: (i, 0)),
        out_shape=jax.ShapeDtypeStruct((n, d), BF16),
        compiler_params=_cparams(("parallel",)),
        name="rms_cast",
    )(x, g.reshape(1, d))


def _in_proj_kernel(rows_ref, a_ref, w_ref, o_ref, wb_sc):
    @pl.when(pl.program_id(1) == 0)
    def _():
        wb_sc[...] = w_ref[0].astype(BF16)

    o_ref[...] = lax.dot_general(a_ref[...], wb_sc[...], (((1,), (1,)), ((), ())),
                                 preferred_element_type=F32)


def _in_proj(a, w_t, layer, w_rows, tm=512, tn=1024):
    m, k = a.shape
    nblk = len(w_rows)
    grid_spec = pltpu.PrefetchScalarGridSpec(
        num_scalar_prefetch=1,
        grid=(nblk, m // tm),
        in_specs=[pl.BlockSpec((tm, k), lambda j, i, r: (i, 0)),
                  pl.BlockSpec((pl.Element(1), pl.Element(tn), pl.Element(k)),
                               lambda j, i, r: (layer, r[j] * 64, 0))],
        out_specs=pl.BlockSpec((tm, tn), lambda j, i, r: (i, j)),
        scratch_shapes=[pltpu.VMEM((tn, k), BF16)],
    )
    return pl.pallas_call(
        _in_proj_kernel,
        grid_spec=grid_spec,
        out_shape=jax.ShapeDtypeStruct((m, nblk * tn), F32),
        compiler_params=_cparams(("arbitrary", "arbitrary")),
        name="in_proj",
    )(jnp.asarray([r // 64 for r in w_rows], jnp.int32), a, w_t)


def _mla_prep_kernel(p_ref, pos_ref, inv_ref, qn_ref, kvn_ref, wq_ref, wuk_ref,
                     qc_ref, rows_ref, kc_ref):
    x = p_ref[...]
    qd = x[:, 0:MLA_QR]
    kvd = x[:, MLA_QR:MLA_QR + MLA_KVR]
    kr = x[:, 768:832]
    krsw = x[:, 896:960]
    qn = qd * lax.rsqrt(jnp.mean(qd * qd, axis=-1, keepdims=True) + NORM_EPS) * qn_ref[...]
    q = _dot(qn, wq_ref[...])
    ang = pos_ref[...] * inv_ref[...]
    cos = jnp.cos(ang)
    sin = jnp.sin(ang)
    hw = MLA_HEADS * 128
    for h in range(MLA_HEADS):
        ql = _dot(q[:, 128 * h:128 * (h + 1)], wuk_ref[h])
        pe = (q[:, hw + 128 * h:hw + 128 * (h + 1)] * cos
              + q[:, 2 * hw + 128 * h:2 * hw + 128 * (h + 1)] * sin)
        qc_ref[h, :, 0:MLA_KVR] = (ql * MLA_SCALE).astype(BF16)
        qc_ref[h, :, MLA_KVR:MLA_QK] = (pe * MLA_SCALE).astype(BF16)
    ckv = kvd * lax.rsqrt(jnp.mean(kvd * kvd, axis=-1, keepdims=True) + NORM_EPS) * kvn_ref[...]
    kpe = kr * cos[:, 0:64] + krsw * sin[:, 0:64]
    rows_ref[:, 0:MLA_KVR] = ckv
    rows_ref[:, MLA_KVR:MLA_CACHE] = kpe
    kc_ref[:, 0:MLA_KVR] = ckv.astype(BF16)
    kc_ref[:, MLA_KVR:MLA_CACHE] = kpe.astype(BF16)
    kc_ref[:, MLA_CACHE:MLA_QK] = jnp.zeros((x.shape[0], MLA_QK - MLA_CACHE), BF16)


def _mla_prep(proj, pos, inv128, q_norm, kv_norm, wq, wuk, tq=256):
    n = proj.shape[0]
    return pl.pallas_call(
        _mla_prep_kernel,
        grid=(n // tq,),
        in_specs=[pl.BlockSpec((tq, 1024), lambda i: (i, 0)),
                  pl.BlockSpec((tq, 1), lambda i: (i, 0)),
                  pl.BlockSpec((1, 128), lambda i: (0, 0)),
                  pl.BlockSpec((1, MLA_QR), lambda i: (0, 0)),
                  pl.BlockSpec((1, MLA_KVR), lambda i: (0, 0)),
                  pl.BlockSpec(wq.shape, lambda i: (0, 0)),
                  pl.BlockSpec(wuk.shape, lambda i: (0, 0, 0))],
        out_specs=[pl.BlockSpec((MLA_HEADS, tq, MLA_QK), lambda i: (0, i, 0)),
                   pl.BlockSpec((tq, MLA_CACHE), lambda i: (i, 0)),
                   pl.BlockSpec((tq, MLA_QK), lambda i: (i, 0))],
        out_shape=[jax.ShapeDtypeStruct((MLA_HEADS, n, MLA_QK), BF16),
                   jax.ShapeDtypeStruct((n, MLA_CACHE), F32),
                   jax.ShapeDtypeStruct((n, MLA_QK), BF16)],
        compiler_params=_cparams(("parallel",)),
        name="mla_prep",
    )(proj, pos, inv128, q_norm.reshape(1, -1), kv_norm.reshape(1, -1), wq, wuk)


def _attn_prompt_kernel(q_ref, k_ref, g_ref, wuv_ref, o_ref, m_sc, l_sc, acc_sc, *, tq, tk, hg):
    i = pl.program_id(1)
    j = pl.program_id(2)
    last = (i * tq + tq - 1) // tk
    rows = MLA_HEADS * tq

    @pl.when(j == 0)
    def _():
        m_sc[...] = jnp.full(m_sc.shape, NEG_BIG, F32)
        l_sc[...] = jnp.zeros(l_sc.shape, F32)
        acc_sc[...] = jnp.zeros(acc_sc.shape, F32)

    def step(masked):
        k = k_ref[...]
        kv = k[:, 0:MLA_KVR]
        grows = hg * tq
        groups = range(MLA_HEADS // hg)
        rsl = [slice(g * grows, (g + 1) * grows) for g in groups]
        m_prev = [m_sc[rs, :] for rs in rsl]
        l_prev = [l_sc[rs, :] for rs in rsl]
        a_prev = [acc_sc[rs, :] for rs in rsl]
        ss = [lax.dot_general(q_ref[g * hg:(g + 1) * hg].reshape(grows, MLA_QK), k,
                              (((1,), (1,)), ((), ())), preferred_element_type=F32) for g in groups]
        if masked:
            qpos = i * tq + (_iota((grows, 1), 0) & (tq - 1))
            kpos = j * tk + _iota((1, tk), 1)
            keep = kpos <= qpos
            ss = [jnp.where(keep, s, NEG_BIG) for s in ss]
        m_new = [jnp.maximum(m_prev[g], jnp.max(ss[g], axis=-1, keepdims=True)) for g in groups]
        alpha = [jnp.exp(m_prev[g] - m_new[g]) for g in groups]
        ps = [jnp.exp(ss[g] - m_new[g]) for g in groups]
        pv = [jnp.dot(ps[g].astype(BF16), kv, preferred_element_type=F32) for g in groups]
        for g, rs in zip(groups, rsl):
            l_sc[rs, :] = alpha[g] * l_prev[g] + jnp.sum(ps[g], axis=-1, keepdims=True)
            acc_sc[rs, :] = alpha[g] * a_prev[g] + pv[g]
            m_sc[rs, :] = m_new[g]

    @pl.when(j < last)
    def _():
        step(False)

    @pl.when(j == last)
    def _():
        step(True)

    @pl.when(j == last)
    def _():
        o_lat = acc_sc[...] / l_sc[...]
        g = g_ref[...]
        for h in range(MLA_HEADS):
            oh = _dot(o_lat[h * tq:(h + 1) * tq, :], wuv_ref[h])
            o_ref[:, 128 * h:128 * (h + 1)] = (oh * _silu(g[:, 128 * h:128 * (h + 1)])).astype(BF16)


def _attn_prompt(qc, kc, proj, wuv, nb, t, tq=128, tk=512, hg=2):
    nq = t // tq
    nk = t // tk
    assert tk % tq == 0 and MLA_HEADS % hg == 0
    kern = functools.partial(_attn_prompt_kernel, tq=tq, tk=tk, hg=hg)

    def kmap(b, i, j):
        return (b * nk + jnp.minimum(j, (i * tq + tq - 1) // tk), 0)

    return pl.pallas_call(
        kern,
        grid=(nb, nq, nk),
        in_specs=[pl.BlockSpec((MLA_HEADS, tq, MLA_QK), lambda b, i, j: (0, b * nq + i, 0)),
                  pl.BlockSpec((tk, MLA_QK), kmap),
                  pl.BlockSpec((tq, 2048), lambda b, i, j: (b * nq + i, COL_GMLA // 2)),
                  pl.BlockSpec(wuv.shape, lambda b, i, j: (0, 0, 0))],
        out_specs=pl.BlockSpec((tq, 2048), lambda b, i, j: (b * nq + i, 0)),
        out_shape=jax.ShapeDtypeStruct((nb * t, 2048), BF16),
        scratch_shapes=[pltpu.VMEM((MLA_HEADS * tq, 1), F32),
                        pltpu.VMEM((MLA_HEADS * tq, 1), F32),
                        pltpu.VMEM((MLA_HEADS * tq, MLA_KVR), F32)],
        compiler_params=_cparams(("parallel", "parallel", "arbitrary")),
        name="attn_prompt",
    )(qc, kc, proj, wuv)


def _attn_sample_kernel(pt_ref, q_ref, nk_ref, *rest, layer, nseq, npg, dt):
    pages = rest[:nseq * npg]
    o_ref = rest[nseq * npg]
    m_sc, l_sc, acc_sc = rest[nseq * npg + 1:]
    j = pl.program_id(1)
    rows = MLA_HEADS * dt

    @pl.when(j == 0)
    def _():
        for s in range(nseq):
            q = q_ref[s]
            kn = nk_ref[s]
            sc = lax.dot_general(q, kn, (((1,), (1,)), ((), ())), preferred_element_type=F32)
            trow = _iota((rows, 1), 0) & (dt - 1)
            kcol = _iota((1, kn.shape[0]), 1)
            sc = jnp.where(kcol <= trow, sc, NEG_BIG)
            m = jnp.max(sc, axis=-1, keepdims=True)
            p = jnp.exp(sc - m)
            m_sc[s] = m
            l_sc[s] = jnp.sum(p, axis=-1, keepdims=True)
            acc_sc[s] = jnp.dot(p.astype(BF16), kn[:, 0:MLA_KVR], preferred_element_type=F32)

    seqs = range(nseq)
    kts = [jnp.concatenate([pages[s * npg + g][...].astype(BF16) for g in range(npg)], axis=1) for s in seqs]
    scs = [jnp.dot(q_ref[s][:, 0:MLA_CACHE], kts[s], preferred_element_type=F32) for s in seqs]
    m_prev = [m_sc[s] for s in seqs]
    l_prev = [l_sc[s] for s in seqs]
    a_prev = [acc_sc[s] for s in seqs]
    m_new = [jnp.maximum(m_prev[s], jnp.max(scs[s], axis=-1, keepdims=True)) for s in seqs]
    alpha = [jnp.exp(m_prev[s] - m_new[s]) for s in seqs]
    ps = [jnp.exp(scs[s] - m_new[s]) for s in seqs]
    pv = [lax.dot_general(ps[s].astype(BF16), kts[s][0:MLA_KVR, :], (((1,), (1,)), ((), ())),
                          preferred_element_type=F32) for s in seqs]
    for s in seqs:
        l_sc[s] = alpha[s] * l_prev[s] + jnp.sum(ps[s], axis=-1, keepdims=True)
        acc_sc[s] = alpha[s] * a_prev[s] + pv[s]
        m_sc[s] = m_new[s]

    @pl.when(j == pl.num_programs(1) - 1)
    def _():
        for s in range(nseq):
            o_ref[s] = acc_sc[s] / l_sc[s]


def _attn_sample(page_table_flat, q_s, newk, cache, layer, n_pages, dt, nseq=4, npg=8):
    db = q_s.shape[0]
    rows = MLA_HEADS * dt
    ngrp = n_pages // npg
    kern = functools.partial(_attn_sample_kernel, layer=layer, nseq=nseq, npg=npg, dt=dt)

    def page_spec(s, g):
        def imap(bb, j, pt):
            return (layer, pt[(bb * nseq + s) * n_pages + j * npg + g], 0, 0)
        return pl.BlockSpec((None, None, MLA_CACHE, PAGE), imap)

    in_specs = [pl.BlockSpec((nseq, rows, MLA_QK), lambda bb, j, pt: (bb, 0, 0)),
                pl.BlockSpec((nseq, 16, MLA_QK), lambda bb, j, pt: (bb, 0, 0))]
    in_specs += [page_spec(s, g) for s in range(nseq) for g in range(npg)]
    grid_spec = pltpu.PrefetchScalarGridSpec(
        num_scalar_prefetch=1,
        grid=(db // nseq, ngrp),
        in_specs=in_specs,
        out_specs=pl.BlockSpec((nseq, rows, MLA_KVR), lambda bb, j, pt: (bb, 0, 0)),
        scratch_shapes=[pltpu.VMEM((nseq, rows, 1), F32),
                        pltpu.VMEM((nseq, rows, 1), F32),
                        pltpu.VMEM((nseq, rows, MLA_KVR), F32)],
    )
    return pl.pallas_call(
        kern,
        grid_spec=grid_spec,
        out_shape=jax.ShapeDtypeStruct((db, rows, MLA_KVR), F32),
        compiler_params=_cparams(("parallel", "arbitrary")),
        name="attn_sample",
    )(page_table_flat, q_s, newk, *([cache] * (nseq * npg)))


def _oproj_gate_kernel(o_ref, g_ref, wuv_ref, out_ref):
    g = g_ref[...]
    for h in range(MLA_HEADS):
        oh = _dot(o_ref[h], wuv_ref[h])
        out_ref[:, 128 * h:128 * (h + 1)] = (oh * _silu(g[:, 128 * h:128 * (h + 1)])).astype(BF16)


def _oproj_gate(o_lat, proj, wuv, tm=256):
    n = o_lat.shape[1]
    return pl.pallas_call(
        _oproj_gate_kernel,
        grid=(n // tm,),
        in_specs=[pl.BlockSpec((MLA_HEADS, tm, MLA_KVR), lambda i: (0, i, 0)),
                  pl.BlockSpec((tm, 2048), lambda i: (i, COL_GMLA // 2)),
                  pl.BlockSpec(wuv.shape, lambda i: (0, 0, 0))],
        out_specs=pl.BlockSpec((tm, 2048), lambda i: (i, 0)),
        out_shape=jax.ShapeDtypeStruct((n, 2048), BF16),
        compiler_params=_cparams(("parallel",)),
        name="oproj_gate",
    )(o_lat, proj, wuv)


def _hgrn_kernel(lb_ref, gn_ref, q_ref, f_ref, i_ref, g_ref, *rest, layer, c, valid, has_init):
    if has_init:
        s0_ref, o_ref, sout_ref, st_sc = rest
    else:
        o_ref, sout_ref, st_sc = rest
    ci = pl.program_id(1)

    @pl.when(ci == 0)
    def _():
        if has_init:
            for h in range(HG_HEADS):
                st_sc[h] = s0_ref[0, h].T
        else:
            st_sc[...] = jnp.zeros(st_sc.shape, F32)

    lbx = lb_ref[...]
    e = jnp.exp(lbx - jnp.max(lbx, axis=0, keepdims=True))
    sm = e / jnp.sum(e, axis=0, keepdims=True)
    lb = jnp.zeros((1, lbx.shape[1]), F32)
    for l in range(1, layer + 1):
        lb = lb + sm[l:l + 1, :]

    z = f_ref[...]
    q_all = _silu(q_ref[...])
    f = lb + (1.0 - lb) * jax.nn.sigmoid(z)
    logf_all = jnp.log(jnp.maximum(f, HG_F_MIN))
    k_all = (1.0 - lb) * jax.nn.sigmoid(-z)
    v_all = i_ref[...]
    t_col = _iota((c, 1), 0)
    if valid < c:
        live = t_col < valid
        logf_all = jnp.where(live, logf_all, 0.0)
        k_all = jnp.where(live, k_all, 0.0)

    t_r = _iota((c, c), 0)
    t_c = _iota((c, c), 1)
    tri = (t_c <= t_r).astype(F32)
    m4 = t_r & 3
    lvl2 = (((m4 == 2) & (t_c == t_r)) | ((m4 == 3) & ((t_c == t_r) | (t_c == t_r - 1)))
            | ((m4 == 0) & (t_c == t_r + 1))).astype(F32)
    cum_all = _dot_split(tri, logf_all)
    e2_all = _dot_split(lvl2, logf_all) if c >= 4 else None

    for h in range(HG_HEADS):
        sl = slice(128 * h, 128 * (h + 1))
        q, k, v, logf, cum = q_all[:, sl], k_all[:, sl], v_all[:, sl], logf_all[:, sl], cum_all[:, sl]
        a = jnp.zeros((c, c), F32)
        half = c // 2
        while half >= 1:
            upper = (t_col & (2 * half - 1)) >= half
            if half >= 4:
                nblk = c // (2 * half)
                c3 = cum.reshape(nblk, 2 * half, 128)
                ref = jnp.broadcast_to(c3[:, half - 1:half, :], (nblk, 2 * half, 128)).reshape(c, 128)
                eq = jnp.where(upper, cum - ref, NEG_BIG)
                ek = jnp.where(upper, NEG_BIG, ref - cum)
            elif half == 2:
                e2 = e2_all[:, sl]
                eq = jnp.where(upper, e2, NEG_BIG)
                ek = jnp.where(upper, NEG_BIG, e2)
            else:
                eq = jnp.where(upper, logf, NEG_BIG)
                ek = jnp.where(upper, NEG_BIG, 0.0)
            prod = _dot_nt(q * jnp.exp(eq), k * jnp.exp(ek))
            shift = int(math.log2(2 * half))
            a = a + jnp.where((t_r >> shift) == (t_c >> shift), prod, 0.0)
            half //= 2
        a = jnp.where(t_r == t_c, jnp.sum(q * k, axis=-1, keepdims=True), a)
        st = st_sc[h]
        o = _dot(a, v) + _dot_nt(q * jnp.exp(cum), st)
        last = cum[c - 1:c, :]
        st_sc[h] = st * jnp.exp(last) + _dot_tn(v, k * jnp.exp(last - cum))
        o = o * lax.rsqrt(jnp.mean(o * o, axis=-1, keepdims=True) + NORM_EPS) * gn_ref[...]
        o_ref[:, sl] = (o * _silu(g_ref[:, sl])).astype(o_ref.dtype)

    @pl.when(ci == pl.num_programs(1) - 1)
    def _():
        for h in range(HG_HEADS):
            sout_ref[0, h] = st_sc[h].T


def _hgrn(proj, lb_raw, g_norm, s0, layer, nseq, t, c, valid):
    nc = t // c
    has_init = s0 is not None
    kern = functools.partial(_hgrn_kernel, layer=layer, c=c, valid=valid, has_init=has_init)

    def col(cb):
        return pl.BlockSpec((c, 1024), lambda b, ci: (b * nc + ci, cb))

    in_specs = [pl.BlockSpec((DEPTH, 1024), lambda b, ci: (0, 0)),
                pl.BlockSpec((1, HG_D), lambda b, ci: (0, 0)),
                col(COL_HGQ), col(COL_HGF), col(COL_HGI), col(COL_GHG)]
    args = [lb_raw, g_norm.reshape(1, HG_D), proj, proj, proj, proj]
    if has_init:
        in_specs.append(pl.BlockSpec((None, 1, HG_HEADS, HG_D, HG_D), lambda b, ci: (layer, b, 0, 0, 0)))
        args.append(s0)
    return pl.pallas_call(
        kern,
        grid=(nseq, nc),
        in_specs=in_specs,
        out_specs=[pl.BlockSpec((c, 1024), lambda b, ci: (b * nc + ci, 0)),
                   pl.BlockSpec((1, HG_HEADS, HG_D, HG_D), lambda b, ci: (b, 0, 0, 0))],
        out_shape=[jax.ShapeDtypeStruct((nseq * t, 1024), BF16 if c % 16 == 0 else F32),
                   jax.ShapeDtypeStruct((nseq, HG_HEADS, HG_D, HG_D), F32)],
        scratch_shapes=[pltpu.VMEM((HG_HEADS, HG_D, HG_D), F32)],
        compiler_params=_cparams(("parallel", "arbitrary")),
        name="hgrn",
    )(*args)


def _rwkv_kernel(xr_ref, xk_ref, xv_ref, xs_ref, g_ref, mu_ref, w0_ref, wup_ref, a0_ref, aup_ref,
                 kk_ref, ka_ref, rk_ref, gnw_ref, gnb_ref, *rest, c, nsb, valid, has_init):
    if has_init:
        sh0_ref, s0_ref, o_ref, sout_ref = rest
    else:
        o_ref, sout_ref, s_sc, carry_sc = rest
        ci = pl.program_id(1)
    npair = RW_HEADS // 2
    rows = nsb * c
    lc = int(math.log2(c))
    w = RW_WIDTH

    if not has_init:
        @pl.when(ci == 0)
        def _():
            carry_sc[...] = jnp.zeros(carry_sc.shape, F32)
            s_sc[...] = jnp.zeros(s_sc.shape, F32)

    t_col = _iota((rows, 1), 0)
    first = (t_col & (c - 1)) == 0
    if has_init:
        sh0 = jnp.broadcast_to(sh0_ref[...], (nsb, c, RW_SHIFT_PAD)).reshape(rows, RW_SHIFT_PAD)

    def shifted(x, lo, hi):
        before = sh0[:, lo:hi] if has_init else carry_sc[0:1, lo:hi]
        prev = jnp.where(first, before, pltpu.roll(x, 1, 0))
        return x + mu_ref[:, lo:hi] * (prev - x)

    xr, xk, xv = xr_ref[...], xk_ref[...], xv_ref[...]
    xs = xs_ref[...]
    xw = xs[:, 768:896]
    xa = xs[:, 896:1024]
    r = shifted(xr, 0, w)
    k = shifted(xk, w, 2 * w)
    v = shifted(xv, 2 * w, 3 * w)
    wl = shifted(xw, 3 * w, 3 * w + 128)
    al = shifted(xa, 3 * w + 128, 3 * w + 256)
    if not has_init:
        carry_sc[0:1, 0:w] = xr[c - 1:c, :]
        carry_sc[0:1, w:2 * w] = xk[c - 1:c, :]
        carry_sc[0:1, 2 * w:3 * w] = xv[c - 1:c, :]
        carry_sc[0:1, 3 * w:3 * w + 128] = xw[c - 1:c, :]
        carry_sc[0:1, 3 * w + 128:3 * w + 256] = xa[c - 1:c, :]

    zw = w0_ref[...] + _dot(jnp.tanh(wl), wup_ref[...])
    log_w = -math.exp(-0.5) * jax.nn.sigmoid(zw)
    ag = jax.nn.sigmoid(a0_ref[...] + _dot(al, aup_ref[...]))

    seg_b = (_iota((128, 128), 0) >> 6) == (_iota((128, 128), 1) >> 6)
    seg = seg_b.astype(BF16)

    def seg_sum(x):
        return jnp.concatenate(
            [jnp.dot(x[:, 128 * p:128 * (p + 1)].astype(BF16), seg, preferred_element_type=F32)
             for p in range(npair)], axis=1)

    kk = k * kk_ref[...]
    kk = kk * lax.rsqrt(jnp.maximum(seg_sum(kk * kk), 1e-24))
    kf = k * (1.0 + (ag - 1.0) * ka_ref[...])
    if valid < c:
        live = (t_col & (c - 1)) < valid
        kk = jnp.where(live, kk, 0.0)
        kf = jnp.where(live, kf, 0.0)
        log_w = jnp.where(live, log_w, 0.0)

    t_r = _iota((rows, rows), 0)
    t_c = _iota((rows, rows), 1)
    tri = ((t_c <= t_r) & ((t_r >> lc) == (t_c >> lc))).astype(F32)
    eye = (t_c == t_r).astype(F32)
    cum = _dot_split(tri, log_w)
    e_in = jnp.exp(cum)
    e_out = jnp.exp(-cum)
    at_all = -kk * jnp.exp(cum - log_w)
    bt_all = kk * ag * e_out
    kt_all = kf * e_out
    rt_all = r * e_in
    r2 = 2 * rows
    a_r = _iota((r2, r2), 0)
    a_c = _iota((r2, r2), 1)
    tr = a_r & (rows - 1)
    tc = a_c & (rows - 1)
    m4 = ((tr >> lc) == (tc >> lc)) & ((tc < tr) | ((a_r >= rows) & (tc == tr)))
    m4_right = m4 & (a_c >= rows)
    head_lo = _iota((1, 128), 1) < RW_HEAD
    nsteps = 0
    cover = 2
    while cover < valid:
        cover *= 2
        nsteps += 1

    def pair(x, p):
        return x[:, 128 * p:128 * (p + 1)]

    if has_init:
        z64 = jnp.zeros((RW_HEAD, RW_HEAD), F32)
        states = [[jnp.concatenate([jnp.concatenate([s0_ref[s, 2 * p], z64], axis=1),
                                    jnp.concatenate([z64, s0_ref[s, 2 * p + 1]], axis=1)], axis=0)
                   for p in range(npair)] for s in range(nsb)]
    else:
        states = [[s_sc[p] for p in range(npair)]]

    ar_at, ar_rt = [], []
    for p in range(npair):
        at_p, rt_p = pair(at_all, p), pair(rt_all, p)
        pa, pr = [], []
        for s in range(nsb):
            rs = slice(s * c, (s + 1) * c)
            ar = _dot_nt(jnp.concatenate([at_p[rs], rt_p[rs]], axis=0), states[s][p])
            pa.append(ar[0:c])
            pr.append(ar[c:2 * c])
        ar_at.append(jnp.concatenate(pa, axis=0) if nsb > 1 else pa[0])
        ar_rt.append(jnp.concatenate(pr, axis=0) if nsb > 1 else pr[0])

    heads = [(p, hh) for p in range(npair) for hh in range(2)]
    hmask = [head_lo if hh == 0 else jnp.logical_not(head_lo) for _, hh in heads]
    labs, mrbs, rhss, ykv = [], [], [], []
    for (p, hh), hm in zip(heads, hmask):
        lhs = jnp.concatenate([jnp.where(hm, pair(at_all, p), 0.0), jnp.where(hm, pair(rt_all, p), 0.0)], axis=0)
        rhs_t = jnp.concatenate([pair(bt_all, p), pair(kt_all, p)], axis=0)
        prod = _dot_nt(lhs, rhs_t)
        vp = pair(v, p)
        lkv = _dot(jnp.where(m4_right, prod, 0.0), jnp.concatenate([vp, vp], axis=0))
        m4p = jnp.where(m4, prod, 0.0)
        labs.append(m4p[0:rows, 0:rows])
        mrbs.append(m4p[rows:r2, 0:rows])
        rhss.append(jnp.where(hm, ar_at[p] + lkv[0:rows], 0.0))
        ykv.append(jnp.where(hm, lkv[rows:r2], 0.0))

    xs_ = [eye + lab for lab in labs]
    pws = labs
    for _ in range(nsteps):
        pws = [_dot(pw, pw) for pw in pws]
        xs_ = [x + _dot(x, pw) for x, pw in zip(xs_, pws)]

    us = [_dot(x, rhs) for x, rhs in zip(xs_, rhss)]
    ys = [_dot(mrb, u) + yk for mrb, u, yk in zip(mrbs, us, ykv)]
    u_pair = [us[2 * p] + us[2 * p + 1] for p in range(npair)]
    y = jnp.concatenate([ar_rt[p] + ys[2 * p] + ys[2 * p + 1] for p in range(npair)], axis=1)

    for p in range(npair):
        bt_p, kt_p, vp, up = pair(bt_all, p), pair(kt_all, p), pair(v, p), u_pair[p]
        for s in range(nsb):
            rs = slice(s * c, (s + 1) * c)
            pc = pair(e_in, p)[s * c + c - 1:s * c + c, :]
            s_new = states[s][p] * pc + _dot_tn(jnp.concatenate([up[rs], vp[rs]], axis=0),
                                                jnp.concatenate([bt_p[rs] * pc, kt_p[rs] * pc], axis=0))
            s_new = jnp.where(seg_b, s_new, 0.0)
            if has_init:
                sout_ref[s, 2 * p] = s_new[0:RW_HEAD, 0:RW_HEAD]
                sout_ref[s, 2 * p + 1] = s_new[RW_HEAD:128, RW_HEAD:128]
            else:
                s_sc[p] = s_new

    mean = seg_sum(y) * (1.0 / RW_HEAD)
    d = y - mean
    var = seg_sum(d * d) * (1.0 / RW_HEAD)
    yn = d * lax.rsqrt(var + GN_EPS) * gnw_ref[...] + gnb_ref[...]
    yn = yn + seg_sum(r * kf * rk_ref[...]) * v
    o_ref[...] = (yn * _silu(g_ref[...])).astype(o_ref.dtype)

    if not has_init:
        @pl.when(ci == pl.num_programs(1) - 1)
        def _():
            for p in range(npair):
                s_bd = s_sc[p]
                sout_ref[0, 2 * p] = s_bd[0:RW_HEAD, 0:RW_HEAD]
                sout_ref[0, 2 * p + 1] = s_bd[RW_HEAD:128, RW_HEAD:128]


def _rwkv(proj, small, prm, sh0, s0, nseq, t, c, nsb, valid):
    nc = t // c
    has_init = s0 is not None
    assert (nc == 1) if has_init else (nsb == 1)
    rows = nsb * c
    kern = functools.partial(_rwkv_kernel, c=c, nsb=nsb, valid=valid, has_init=has_init)

    def col(cb):
        return pl.BlockSpec((rows, 1024), lambda b, ci: (b * nc + ci, cb))

    def const(shape):
        return pl.BlockSpec(shape, lambda b, ci: (0,) * len(shape))

    in_specs = [col(COL_RWR), col(COL_RWK), col(COL_RWV), col(0), col(COL_GRW),
                const((1, RW_SHIFT_PAD)), const((1, 1024)), const((128, 1024)), const((1, 1024)),
                const((128, 1024)), const((1, 1024)), const((1, 1024)), const((1, 1024)),
                const((1, 1024)), const((1, 1024))]
    args = [proj, proj, proj, small, proj, prm["mu"], prm["w0"], prm["w_up"], prm["a0"], prm["a_up"],
            prm["k_k"], prm["k_a"], prm["r_k"], prm["gn_w"], prm["gn_b"]]
    scratch = []
    if has_init:
        in_specs += [pl.BlockSpec((nsb, 1, RW_SHIFT_PAD), lambda b, ci: (b, 0, 0)),
                     pl.BlockSpec((nsb, RW_HEADS, RW_HEAD, RW_HEAD), lambda b, ci: (b, 0, 0, 0))]
        args += [sh0, s0]
    else:
        scratch = [pltpu.VMEM((RW_HEADS // 2, 128, 128), F32), pltpu.VMEM((8, RW_SHIFT_PAD), F32)]
    return pl.pallas_call(
        kern,
        grid=(nseq // nsb, nc),
        in_specs=in_specs,
        out_specs=[pl.BlockSpec((rows, 1024), lambda b, ci: (b * nc + ci, 0)),
                   pl.BlockSpec((nsb, RW_HEADS, RW_HEAD, RW_HEAD), lambda b, ci: (b, 0, 0, 0))],
        out_shape=[jax.ShapeDtypeStruct((nseq * t, 1024), BF16),
                   jax.ShapeDtypeStruct((nseq, RW_HEADS, RW_HEAD, RW_HEAD), F32)],
        scratch_shapes=scratch,
        compiler_params=_cparams(("parallel", "arbitrary")),
        name="rwkv",
    )(*args)


def _out_proj_kernel(x_ref, m_ref, hg_ref, rw_ref, w_ref, g_ref, o_ref):
    acc = _dot(m_ref[...], w_ref[0:2048, :])
    acc = acc + _dot(hg_ref[...], w_ref[2048:3072, :])
    acc = acc + _dot(rw_ref[...], w_ref[3072:4096, :])
    y = acc * lax.rsqrt(jnp.mean(acc * acc, axis=-1, keepdims=True) + NORM_EPS) * g_ref[...]
    o_ref[...] = x_ref[...] + y


def _out_proj(x, m_mla, m_hg, m_rw, w_out, post_norm, tm=256):
    n = x.shape[0]
    return pl.pallas_call(
        _out_proj_kernel,
        grid=(n // tm,),
        in_specs=[pl.BlockSpec((tm, 2048), lambda i: (i, 0)),
                  pl.BlockSpec((tm, 2048), lambda i: (i, 0)),
                  pl.BlockSpec((tm, 1024), lambda i: (i, 0)),
                  pl.BlockSpec((tm, 1024), lambda i: (i, 0)),
                  pl.BlockSpec((4096, 2048), lambda i: (0, 0), pipeline_mode=pl.Buffered(1)),
                  pl.BlockSpec((1, 2048), lambda i: (0, 0))],
        out_specs=pl.BlockSpec((tm, 2048), lambda i: (i, 0)),
        out_shape=jax.ShapeDtypeStruct((n, 2048), F32),
        compiler_params=_cparams(("parallel",)),
        name="out_proj",
    )(x, m_mla, m_hg, m_rw, w_out, post_norm.reshape(1, -1))


def _prep_w_small(w_in_t):
    kr = w_in_t[:, 768:832]
    kr_sw = jnp.concatenate([-kr[:, 32:], kr[:, :32]], axis=1)
    wl = w_in_t[:, RW_X_ROW + 3072:RW_X_ROW + 3136]
    al = w_in_t[:, RW_X_ROW + 3136:RW_X_ROW + 3200]
    return jnp.concatenate([w_in_t[:, 0:832], wl, kr_sw, al], axis=1)


def _prep_wq(w_q_up):
    d, r, h, _ = w_q_up.shape
    nope = w_q_up[..., :MLA_NOPE].reshape(d, r, h * MLA_NOPE)
    pe = w_q_up[..., MLA_NOPE:]
    pe_sw = jnp.concatenate([-pe[..., 32:], pe[..., :32]], axis=-1)
    pad = lambda a: jnp.pad(a, ((0, 0), (0, 0), (0, 0), (0, 64))).reshape(d, r, h * 128)
    return jnp.concatenate([nope, pad(pe), pad(pe_sw)], axis=-1).astype(BF16)


def _reorder_shift(v):
    z = jnp.zeros(v.shape[:-1] + (64,), v.dtype)
    return jnp.concatenate([v[..., :3072], z, v[..., 3072:3136], z, v[..., 3136:3200]], axis=-1)


def kernel(x_prompt, x_sample, cache_mla, page_table, state_hgrn, state_rwkv, state_rwkv_shift, pre_norm, post_norm, w_in, w_out, mla_q_norm, mla_kv_norm, mla_w_q_up, mla_w_kv_up, hg_lower_bound, hg_g_norm, rw_mu, rw_w0, rw_w_up, rw_a0, rw_a_up, rw_k_k, rw_k_a, rw_r_k, rw_gn_w, rw_gn_b):
    nb, t, d = x_prompt.shape
    db, dt, _ = x_sample.shape
    n_pages = page_table.shape[1]
    past_len = n_pages * PAGE
    sp = SAMPLE_PAD

    w_in_t = jnp.swapaxes(w_in, 1, 2)
    w_small_t = _prep_w_small(w_in_t)
    wq_r = _prep_wq(mla_w_q_up)
    wuk = jnp.transpose(mla_w_kv_up[..., :MLA_NOPE], (0, 2, 3, 1)).astype(BF16)
    wuv = jnp.transpose(mla_w_kv_up[..., MLA_NOPE:], (0, 2, 1, 3)).astype(BF16)
    w_out_b = w_out.astype(BF16)
    zpad = jnp.zeros((DEPTH, 64, RW_WIDTH), F32)
    w_up_p = jnp.concatenate([zpad, rw_w_up], axis=1).astype(BF16)
    a_up_p = jnp.concatenate([zpad, rw_a_up], axis=1).astype(BF16)
    mu_r = _reorder_shift(rw_mu)
    sh0_r = _reorder_shift(state_rwkv_shift).reshape(DEPTH, db, 1, RW_SHIFT_PAD)
    half = MLA_ROPE // 2
    inv = ROPE_BASE ** (-jnp.arange(half, dtype=F32) / half)
    inv128 = jnp.tile(inv, 4).reshape(1, 128)
    pos_p = jnp.tile(jnp.arange(t, dtype=F32), nb).reshape(nb * t, 1)
    pos_s = jnp.tile(past_len + jnp.arange(sp, dtype=F32), db).reshape(db * sp, 1)
    pt_flat = page_table.reshape(-1)
    cache_t = jnp.swapaxes(cache_mla, 2, 3)

    xp = x_prompt.reshape(nb * t, d)
    xs = jnp.pad(x_sample, ((0, 0), (0, sp - dt), (0, 0))).reshape(db * sp, d)

    rows_p, rows_s, hg_p, hg_s, rw_p, rw_s, sh_p, sh_s = [], [], [], [], [], [], [], []
    for l in range(DEPTH):
        prm = {"mu": mu_r[l:l + 1], "w0": rw_w0[l:l + 1], "w_up": w_up_p[l], "a0": rw_a0[l:l + 1],
               "a_up": a_up_p[l], "k_k": rw_k_k[l:l + 1], "k_a": rw_k_a[l:l + 1],
               "r_k": rw_r_k[l].reshape(1, RW_WIDTH), "gn_w": rw_gn_w[l:l + 1], "gn_b": rw_gn_b[l:l + 1]}

        h = _rms_cast(xp, pre_norm[l])
        small = _in_proj(h, w_small_t, l, (0,))
        proj = _in_proj(h, w_in_t, l, MAIN_W_ROWS)
        qc, rows, kc = _mla_prep(small, pos_p, inv128, mla_q_norm[l], mla_kv_norm[l], wq_r[l], wuk[l])
        m_mla = _attn_prompt(qc, kc, proj, wuv[l], nb, t)
        m_hg, hg_state = _hgrn(proj, hg_lower_bound, hg_g_norm[l], None, l, nb, t, 128, 128)
        m_rw, rw_state = _rwkv(proj, small, prm, None, None, nb, t, 64, 1, 64)
        rw_off = COL_RWR * 1024
        sh = jnp.concatenate([proj.reshape(nb, t, D_MAIN)[:, t - 1, rw_off:rw_off + 3072],
                              small.reshape(nb, t, 1024)[:, t - 1, 832:896],
                              small.reshape(nb, t, 1024)[:, t - 1, 960:1024]], axis=-1)
        xp = _out_proj(xp, m_mla, m_hg, m_rw, w_out_b[l], post_norm[l])
        rows_p.append(rows.reshape(nb, t, MLA_CACHE)); hg_p.append(hg_state); rw_p.append(rw_state); sh_p.append(sh)

        h = _rms_cast(xs, pre_norm[l])
        small = _in_proj(h, w_small_t, l, (0,))
        proj = _in_proj(h, w_in_t, l, MAIN_W_ROWS)
        qc, rows, kc = _mla_prep(small, pos_s, inv128, mla_q_norm[l], mla_kv_norm[l], wq_r[l], wuk[l])
        q_s = qc.reshape(MLA_HEADS, db, sp, MLA_QK)[:, :, :dt]
        q_s = jnp.transpose(q_s, (1, 0, 2, 3)).reshape(db, MLA_HEADS * dt, MLA_QK)
        newk = jnp.pad(kc.reshape(db, sp, MLA_QK)[:, :dt], ((0, 0), (0, 16 - dt), (0, 0)))
        o_lat = _attn_sample(pt_flat, q_s, newk, cache_t, l, n_pages, dt)
        o_lat = jnp.transpose(o_lat.reshape(db, MLA_HEADS, dt, MLA_KVR), (1, 0, 2, 3))
        o_lat = jnp.pad(o_lat, ((0, 0), (0, 0), (0, sp - dt), (0, 0))).reshape(MLA_HEADS, db * sp, MLA_KVR)
        m_mla = _oproj_gate(o_lat, proj, wuv[l])
        m_hg, hg_state = _hgrn(proj, hg_lower_bound, hg_g_norm[l], state_hgrn, l, db, sp, sp, dt)
        m_rw, rw_state = _rwkv(proj, small, prm, sh0_r[l], state_rwkv[l], db, sp, sp, 8, dt)
        pr = proj.reshape(db, sp, D_MAIN)[:, dt - 1]
        ps = small.reshape(db, sp, 1024)[:, dt - 1]
        sh = jnp.concatenate([pr[:, rw_off:rw_off + 3072], ps[:, 832:896], ps[:, 960:1024]], axis=-1)
        xs = _out_proj(xs, m_mla, m_hg, m_rw, w_out_b[l], post_norm[l])
        rows_s.append(rows.reshape(db, sp, MLA_CACHE)[:, :dt]); hg_s.append(hg_state); rw_s.append(rw_state); sh_s.append(sh)

    y_p = xp.reshape(nb, t, d)
    y_s = xs.reshape(db, sp, d)[:, :dt]
    return (y_p, y_s, jnp.stack(rows_p), jnp.stack(rows_s), jnp.stack(hg_p), jnp.stack(hg_s),
            jnp.stack(rw_p), jnp.stack(rw_s), jnp.stack(sh_p), jnp.stack(sh_s))
```

```python
import functools
import math

import jax
import jax.numpy as jnp
import numpy as np
from jax import lax
from jax.experimental import pallas as pl
from jax.experimental.pallas import tpu as pltpu

F32 = jnp.float32
BF16 = jnp.bfloat16

D_MODEL = 2048
DEPTH = 4
PAGE = 128
MLA_HEADS = 16
MLA_NOPE = 128
MLA_ROPE = 64
MLA_V = 128
MLA_QR = 512
MLA_KVR = 256
MLA_CACHE = MLA_KVR + MLA_ROPE
MLA_QK = MLA_KVR + 128
MLA_SCALE = 1.0 / math.sqrt(MLA_NOPE + MLA_ROPE)
ROPE_BASE = 10000.0
NEG_BIG = -1e30
HG_HEADS = 8
HG_D = 128
HG_F_MIN = 1e-30
RW_HEADS = 16
RW_HEAD = 64
RW_WIDTH = 1024
RW_LORA = 64
RW_SHIFT = 3 * RW_WIDTH + 2 * RW_LORA
RW_SHIFT_PAD = 3 * RW_WIDTH + 256
GN_EPS = 64e-5
NORM_EPS = 1e-6
SAMPLE_PAD = 8
VMEM_LIMIT = 48 * 1024 * 1024

COL_GMLA, COL_HGQ, COL_HGF, COL_HGI, COL_GHG, COL_GRW, COL_RWR, COL_RWK, COL_RWV = (0, 2, 3, 4, 5, 6, 7, 8, 9)
MAIN_W_ROWS = (832, 1856, 2880, 3904, 4928, 5952, 10176, 6976, 8000, 9024)
D_MAIN = 10 * 1024
RW_X_ROW = 6976


def _cparams(sem):
    return pltpu.CompilerParams(dimension_semantics=sem, vmem_limit_bytes=VMEM_LIMIT)


def _dot(a, b):
    return jnp.dot(a.astype(BF16), b.astype(BF16), preferred_element_type=F32)


def _dot_nt(a, b):
    return lax.dot_general(a.astype(BF16), b.astype(BF16), (((1,), (1,)), ((), ())),
                           preferred_element_type=F32)


def _dot_tn(a, b):
    return lax.dot_general(a.astype(BF16), b.astype(BF16), (((0,), (0,)), ((), ())),
                           preferred_element_type=F32)


def _dot_split(m01, x):
    m = m01.astype(BF16)
    hi = x.astype(BF16)
    r1 = x - hi.astype(F32)
    mid = r1.astype(BF16)
    lo = (r1 - mid.astype(F32)).astype(BF16)
    d = lambda p: jnp.dot(m, p, preferred_element_type=F32)
    return d(hi) + d(mid) + d(lo)


def _dot_split_r(x, m01):
    m = m01.astype(BF16)
    hi = x.astype(BF16)
    r1 = x - hi.astype(F32)
    mid = r1.astype(BF16)
    lo = (r1 - mid.astype(F32)).astype(BF16)
    d = lambda p: jnp.dot(p, m, preferred_element_type=F32)
    return d(hi) + d(mid) + d(lo)


def _silu(x):
    return x * jax.nn.sigmoid(x)


def _iota(shape, dim):
    return lax.broadcasted_iota(jnp.int32, shape, dim)


def _rms_cast_kernel(x_ref, g_ref, o_ref):
    x = x_ref[...]
    y = x * lax.rsqrt(jnp.mean(x * x, axis=-1, keepdims=True) + NORM_EPS)
    o_ref[...] = (y * g_ref[...]).astype(BF16)


def _rms_cast(x, g, tm=512):
    n, d = x.shape
    return pl.pallas_call(
        _rms_cast_kernel,
        grid=(n // tm,),
        in_specs=[pl.BlockSpec((tm, d), lambda i: (i, 0)),
                  pl.BlockSpec((1, d), lambda i: (0, 0))],
        out_specs=pl.BlockSpec((tm, d), lambda i: (i, 0)),
        out_shape=jax.ShapeDtypeStruct((n, d), BF16),
        compiler_params=_cparams(("parallel",)),
        name="rms_cast",
    )(x, g.reshape(1, d))


def _in_proj_kernel(rows_ref, a_ref, w_ref, o_ref, wb_sc):
    @pl.when(pl.program_id(1) == 0)
    def _():
        wb_sc[...] = w_ref[0].astype(BF16)

    o_ref[...] = lax.dot_general(a_ref[...], wb_sc[...], (((1,), (1,)), ((), ())),
                                 preferred_element_type=F32)


def _in_proj(a, w_t, layer, w_rows, tm=512, tn=1024):
    m, k = a.shape
    nblk = len(w_rows)
    grid_spec = pltpu.PrefetchScalarGridSpec(
        num_scalar_prefetch=1,
        grid=(nblk, m // tm),
        in_specs=[pl.BlockSpec((tm, k), lambda j, i, r: (i, 0)),
                  pl.BlockSpec((pl.Element(1), pl.Element(tn), pl.Element(k)),
                               lambda j, i, r: (layer, r[j] * 64, 0))],
        out_specs=pl.BlockSpec((tm, tn), lambda j, i, r: (i, j)),
        scratch_shapes=[pltpu.VMEM((tn, k), BF16)],
    )
    return pl.pallas_call(
        _in_proj_kernel,
        grid_spec=grid_spec,
        out_shape=jax.ShapeDtypeStruct((m, nblk * tn), F32),
        compiler_params=_cparams(("arbitrary", "arbitrary")),
        name="in_proj",
    )(jnp.asarray([r // 64 for r in w_rows], jnp.int32), a, w_t)


def _mla_prep_kernel(p_ref, pos_ref, inv_ref, qn_ref, kvn_ref, wq_ref, wuk_ref,
                     qc_ref, rows_ref, kc_ref):
    x = p_ref[...]
    qd = x[:, 0:MLA_QR]
    kvd = x[:, MLA_QR:MLA_QR + MLA_KVR]
    kr = x[:, 768:832]
    krsw = x[:, 896:960]
    qn = qd * lax.rsqrt(jnp.mean(qd * qd, axis=-1, keepdims=True) + NORM_EPS) * qn_ref[...]
    q = _dot(qn, wq_ref[...])
    ang = pos_ref[...] * inv_ref[...]
    cos = jnp.cos(ang)
    sin = jnp.sin(ang)
    hw = MLA_HEADS * 128
    for h in range(MLA_HEADS):
        ql = _dot(q[:, 128 * h:128 * (h + 1)], wuk_ref[h])
        pe = (q[:, hw + 128 * h:hw + 128 * (h + 1)] * cos
              + q[:, 2 * hw + 128 * h:2 * hw + 128 * (h + 1)] * sin)
        qc_ref[h, :, 0:MLA_KVR] = (ql * MLA_SCALE).astype(BF16)
        qc_ref[h, :, MLA_KVR:MLA_QK] = (pe * MLA_SCALE).astype(BF16)
    ckv = kvd * lax.rsqrt(jnp.mean(kvd * kvd, axis=-1, keepdims=True) + NORM_EPS) * kvn_ref[...]
    kpe = kr * cos[:, 0:64] + krsw * sin[:, 0:64]
    rows_ref[:, 0:MLA_KVR] = ckv
    rows_ref[:, MLA_KVR:MLA_CACHE] = kpe
    kc_ref[:, 0:MLA_KVR] = ckv.astype(BF16)
    kc_ref[:, MLA_KVR:MLA_CACHE] = kpe.astype(BF16)
    kc_ref[:, MLA_CACHE:MLA_QK] = jnp.zeros((x.shape[0], MLA_QK - MLA_CACHE), BF16)


def _mla_prep(proj, pos, inv128, q_norm, kv_norm, wq, wuk, tq=256):
    n = proj.shape[0]
    return pl.pallas_call(
        _mla_prep_kernel,
        grid=(n // tq,),
        in_specs=[pl.BlockSpec((tq, 1024), lambda i: (i, 0)),
                  pl.BlockSpec((tq, 1), lambda i: (i, 0)),
                  pl.BlockSpec((1, 128), lambda i: (0, 0)),
                  pl.BlockSpec((1, MLA_QR), lambda i: (0, 0)),
                  pl.BlockSpec((1, MLA_KVR), lambda i: (0, 0)),
                  pl.BlockSpec(wq.shape, lambda i: (0, 0)),
                  pl.BlockSpec(wuk.shape, lambda i: (0, 0, 0))],
        out_specs=[pl.BlockSpec((MLA_HEADS, tq, MLA_QK), lambda i: (0, i, 0)),
                   pl.BlockSpec((tq, MLA_CACHE), lambda i: (i, 0)),
                   pl.BlockSpec((tq, MLA_QK), lambda i: (i, 0))],
        out_shape=[jax.ShapeDtypeStruct((MLA_HEADS, n, MLA_QK), BF16),
                   jax.ShapeDtypeStruct((n, MLA_CACHE), F32),
                   jax.ShapeDtypeStruct((n, MLA_QK), BF16)],
        compiler_params=_cparams(("parallel",)),
        name="mla_prep",
    )(proj, pos, inv128, q_norm.reshape(1, -1), kv_norm.reshape(1, -1), wq, wuk)


def _attn_prompt_kernel(q_ref, k_ref, g_ref, wuv_ref, o_ref, m_sc, l_sc, acc_sc, *, tq, tk, hg):
    i = pl.program_id(1)
    j = pl.program_id(2)
    last = (i * tq + tq - 1) // tk
    rows = MLA_HEADS * tq

    @pl.when(j == 0)
    def _():
        m_sc[...] = jnp.full(m_sc.shape, NEG_BIG, F32)
        l_sc[...] = jnp.zeros(l_sc.shape, F32)
        acc_sc[...] = jnp.zeros(acc_sc.shape, F32)

    def step(masked):
        k = k_ref[...]
        kv = k[:, 0:MLA_KVR]
        grows = hg * tq
        groups = range(MLA_HEADS // hg)
        rsl = [slice(g * grows, (g + 1) * grows) for g in groups]
        m_prev = [m_sc[rs, :] for rs in rsl]
        l_prev = [l_sc[rs, :] for rs in rsl]
        a_prev = [acc_sc[rs, :] for rs in rsl]
        ss = [lax.dot_general(q_ref[g * hg:(g + 1) * hg].reshape(grows, MLA_QK), k,
                              (((1,), (1,)), ((), ())), preferred_element_type=F32) for g in groups]
        if masked:
            qpos = i * tq + (_iota((grows, 1), 0) & (tq - 1))
            kpos = j * tk + _iota((1, tk), 1)
            keep = kpos <= qpos
            ss = [jnp.where(keep, s, NEG_BIG) for s in ss]
        m_new = [jnp.maximum(m_prev[g], jnp.max(ss[g], axis=-1, keepdims=True)) for g in groups]
        alpha = [jnp.exp(m_prev[g] - m_new[g]) for g in groups]
        ps = [jnp.exp(ss[g] - m_new[g]) for g in groups]
        pv = [jnp.dot(ps[g].astype(BF16), kv, preferred_element_type=F32) for g in groups]
        for g, rs in zip(groups, rsl):
            l_sc[rs, :] = alpha[g] * l_prev[g] + jnp.sum(ps[g], axis=-1, keepdims=True)
            acc_sc[rs, :] = alpha[g] * a_prev[g] + pv[g]
            m_sc[rs, :] = m_new[g]

    @pl.when(j < last)
    def _():
        step(False)

    @pl.when(j == last)
    def _():
        step(True)

    @pl.when(j == last)
    def _():
        o_lat = acc_sc[...] / l_sc[...]
        g = g_ref[...]
        for h in range(MLA_HEADS):
            oh = _dot(o_lat[h * tq:(h + 1) * tq, :], wuv_ref[h])
            o_ref[:, 128 * h:128 * (h + 1)] = (oh * _silu(g[:, 128 * h:128 * (h + 1)])).astype(BF16)


def _attn_prompt(qc, kc, proj, wuv, nb, t, tq=128, tk=512, hg=2):
    nq = t // tq
    nk = t // tk
    assert tk % tq == 0 and MLA_HEADS % hg == 0
    kern = functools.partial(_attn_prompt_kernel, tq=tq, tk=tk, hg=hg)

    def kmap(b, i, j):
        return (b * nk + jnp.minimum(j, (i * tq + tq - 1) // tk), 0)

    return pl.pallas_call(
        kern,
        grid=(nb, nq, nk),
        in_specs=[pl.BlockSpec((MLA_HEADS, tq, MLA_QK), lambda b, i, j: (0, b * nq + i, 0)),
                  pl.BlockSpec((tk, MLA_QK), kmap),
                  pl.BlockSpec((tq, 2048), lambda b, i, j: (b * nq + i, COL_GMLA // 2)),
                  pl.BlockSpec(wuv.shape, lambda b, i, j: (0, 0, 0))],
        out_specs=pl.BlockSpec((tq, 2048), lambda b, i, j: (b * nq + i, 0)),
        out_shape=jax.ShapeDtypeStruct((nb * t, 2048), BF16),
        scratch_shapes=[pltpu.VMEM((MLA_HEADS * tq, 1), F32),
                        pltpu.VMEM((MLA_HEADS * tq, 1), F32),
                        pltpu.VMEM((MLA_HEADS * tq, MLA_KVR), F32)],
        compiler_params=_cparams(("parallel", "parallel", "arbitrary")),
        name="attn_prompt",
    )(qc, kc, proj, wuv)


def _attn_sample_kernel(pt_ref, q_ref, nk_ref, *rest, layer, nseq, npg, dt):
    pages = rest[:nseq * npg]
    o_ref = rest[nseq * npg]
    m_sc, l_sc, acc_sc = rest[nseq * npg + 1:]
    j = pl.program_id(1)
    rows = MLA_HEADS * dt

    @pl.when(j == 0)
    def _():
        for s in range(nseq):
            q = q_ref[s]
            kn = nk_ref[s]
            sc = lax.dot_general(q, kn, (((1,), (1,)), ((), ())), preferred_element_type=F32)
            trow = _iota((rows, 1), 0) & (dt - 1)
            kcol = _iota((1, kn.shape[0]), 1)
            sc = jnp.where(kcol <= trow, sc, NEG_BIG)
            m = jnp.max(sc, axis=-1, keepdims=True)
            p = jnp.exp(sc - m)
            m_sc[s] = m
            l_sc[s] = jnp.sum(p, axis=-1, keepdims=True)
            acc_sc[s] = jnp.dot(p.astype(BF16), kn[:, 0:MLA_KVR], preferred_element_type=F32)

    seqs = range(nseq)
    kts = [jnp.concatenate([pages[s * npg + g][...].astype(BF16) for g in range(npg)], axis=1) for s in seqs]
    scs = [jnp.dot(q_ref[s][:, 0:MLA_CACHE], kts[s], preferred_element_type=F32) for s in seqs]
    m_prev = [m_sc[s] for s in seqs]
    l_prev = [l_sc[s] for s in seqs]
    a_prev = [acc_sc[s] for s in seqs]
    m_new = [jnp.maximum(m_prev[s], jnp.max(scs[s], axis=-1, keepdims=True)) for s in seqs]
    alpha = [jnp.exp(m_prev[s] - m_new[s]) for s in seqs]
    ps = [jnp.exp(scs[s] - m_new[s]) for s in seqs]
    pv = [lax.dot_general(ps[s].astype(BF16), kts[s][0:MLA_KVR, :], (((1,), (1,)), ((), ())),
                          preferred_element_type=F32) for s in seqs]
    for s in seqs:
        l_sc[s] = alpha[s] * l_prev[s] + jnp.sum(ps[s], axis=-1, keepdims=True)
        acc_sc[s] = alpha[s] * a_prev[s] + pv[s]
        m_sc[s] = m_new[s]

    @pl.when(j == pl.num_programs(1) - 1)
    def _():
        for s in range(nseq):
            o_ref[s] = acc_sc[s] / l_sc[s]


def _attn_sample(page_table_flat, q_s, newk, cache, layer, n_pages, dt, nseq=4, npg=8):
    db = q_s.shape[0]
    rows = MLA_HEADS * dt
    ngrp = n_pages // npg
    kern = functools.partial(_attn_sample_kernel, layer=layer, nseq=nseq, npg=npg, dt=dt)

    def page_spec(s, g):
        def imap(bb, j, pt):
            return (layer, pt[(bb * nseq + s) * n_pages + j * npg + g], 0, 0)
        return pl.BlockSpec((None, None, MLA_CACHE, PAGE), imap)

    in_specs = [pl.BlockSpec((nseq, rows, MLA_QK), lambda bb, j, pt: (bb, 0, 0)),
                pl.BlockSpec((nseq, 16, MLA_QK), lambda bb, j, pt: (bb, 0, 0))]
    in_specs += [page_spec(s, g) for s in range(nseq) for g in range(npg)]
    grid_spec = pltpu.PrefetchScalarGridSpec(
        num_scalar_prefetch=1,
        grid=(db // nseq, ngrp),
        in_specs=in_specs,
        out_specs=pl.BlockSpec((nseq, rows, MLA_KVR), lambda bb, j, pt: (bb, 0, 0)),
        scratch_shapes=[pltpu.VMEM((nseq, rows, 1), F32),
                        pltpu.VMEM((nseq, rows, 1), F32),
                        pltpu.VMEM((nseq, rows, MLA_KVR), F32)],
    )
    return pl.pallas_call(
        kern,
        grid_spec=grid_spec,
        out_shape=jax.ShapeDtypeStruct((db, rows, MLA_KVR), F32),
        compiler_params=_cparams(("parallel", "arbitrary")),
        name="attn_sample",
    )(page_table_flat, q_s, newk, *([cache] * (nseq * npg)))


def _oproj_gate_kernel(o_ref, g_ref, wuv_ref, out_ref):
    g = g_ref[...]
    for h in range(MLA_HEADS):
        oh = _dot(o_ref[h], wuv_ref[h])
        out_ref[:, 128 * h:128 * (h + 1)] = (oh * _silu(g[:, 128 * h:128 * (h + 1)])).astype(BF16)


def _oproj_gate(o_lat, proj, wuv, tm=256):
    n = o_lat.shape[1]
    return pl.pallas_call(
        _oproj_gate_kernel,
        grid=(n // tm,),
        in_specs=[pl.BlockSpec((MLA_HEADS, tm, MLA_KVR), lambda i: (0, i, 0)),
                  pl.BlockSpec((tm, 2048), lambda i: (i, COL_GMLA // 2)),
                  pl.BlockSpec(wuv.shape, lambda i: (0, 0, 0))],
        out_specs=pl.BlockSpec((tm, 2048), lambda i: (i, 0)),
        out_shape=jax.ShapeDtypeStruct((n, 2048), BF16),
        compiler_params=_cparams(("parallel",)),
        name="oproj_gate",
    )(o_lat, proj, wuv)


def _hgrn_kernel(lb_ref, gn_ref, q_ref, f_ref, i_ref, g_ref, *rest, layer, c, nsb, valid, has_init):
    if has_init:
        s0_ref, o_ref, sout_ref = rest
    else:
        o_ref, sout_ref, st_sc = rest
        ci = pl.program_id(1)

        @pl.when(ci == 0)
        def _():
            st_sc[...] = jnp.zeros(st_sc.shape, F32)
    rows = nsb * c
    lc = int(math.log2(c))

    lbx = lb_ref[...]
    e = jnp.exp(lbx - jnp.max(lbx, axis=0, keepdims=True))
    sm = e / jnp.sum(e, axis=0, keepdims=True)
    lb = jnp.zeros((1, lbx.shape[1]), F32)
    for l in range(1, layer + 1):
        lb = lb + sm[l:l + 1, :]

    z = f_ref[...]
    q_all = _silu(q_ref[...])
    f = lb + (1.0 - lb) * jax.nn.sigmoid(z)
    logf_all = jnp.log(jnp.maximum(f, HG_F_MIN))
    k_all = (1.0 - lb) * jax.nn.sigmoid(-z)
    v_all = i_ref[...]
    t_col = _iota((rows, 1), 0)
    if valid < c:
        live = (t_col & (c - 1)) < valid
        logf_all = jnp.where(live, logf_all, 0.0)
        k_all = jnp.where(live, k_all, 0.0)

    t_r = _iota((rows, rows), 0)
    t_c = _iota((rows, rows), 1)
    tri = ((t_c <= t_r) & ((t_r >> lc) == (t_c >> lc))).astype(F32)
    m4 = t_r & 3
    lvl2 = (((m4 == 2) & (t_c == t_r)) | ((m4 == 3) & ((t_c == t_r) | (t_c == t_r - 1)))
            | ((m4 == 0) & (t_c == t_r + 1))).astype(F32)
    cum_all = _dot_split(tri, logf_all)
    e2_all = _dot_split(lvl2, logf_all) if c >= 4 else None
    last_all = jnp.broadcast_to(cum_all.reshape(nsb, c, 1024)[:, c - 1:c, :], (nsb, c, 1024)).reshape(rows, 1024)
    kdec_all = k_all * jnp.exp(last_all - cum_all)
    qdec_all = q_all * jnp.exp(cum_all)

    if has_init:
        states = [[s0_ref[s, h].T for h in range(HG_HEADS)] for s in range(nsb)]
    else:
        states = [[st_sc[h] for h in range(HG_HEADS)]]

    a_mats = []
    for h in range(HG_HEADS):
        sl = slice(128 * h, 128 * (h + 1))
        q, k, logf, cum = q_all[:, sl], k_all[:, sl], logf_all[:, sl], cum_all[:, sl]
        a = jnp.zeros((rows, rows), F32)
        half = c // 2
        while half >= 1:
            upper = (t_col & (2 * half - 1)) >= half
            if half >= 4:
                nblk = rows // (2 * half)
                c3 = cum.reshape(nblk, 2 * half, 128)
                ref = jnp.broadcast_to(c3[:, half - 1:half, :], (nblk, 2 * half, 128)).reshape(rows, 128)
                eq = jnp.where(upper, cum - ref, NEG_BIG)
                ek = jnp.where(upper, NEG_BIG, ref - cum)
            elif half == 2:
                e2 = e2_all[:, sl]
                eq = jnp.where(upper, e2, NEG_BIG)
                ek = jnp.where(upper, NEG_BIG, e2)
            else:
                eq = jnp.where(upper, logf, NEG_BIG)
                ek = jnp.where(upper, NEG_BIG, 0.0)
            prod = _dot_nt(q * jnp.exp(eq), k * jnp.exp(ek))
            shift = int(math.log2(2 * half))
            a = a + jnp.where((t_r >> shift) == (t_c >> shift), prod, 0.0)
            half //= 2
        a_mats.append(jnp.where(t_r == t_c, jnp.sum(q * k, axis=-1, keepdims=True), a))

    hs = [(s, h) for h in range(HG_HEADS) for s in range(nsb)]
    inter = [_dot_nt(qdec_all[s * c:(s + 1) * c, 128 * h:128 * (h + 1)], states[s][h]) for s, h in hs]
    st_new = [states[s][h] * jnp.exp(last_all[s * c:s * c + 1, 128 * h:128 * (h + 1)])
              + _dot_tn(v_all[s * c:(s + 1) * c, 128 * h:128 * (h + 1)],
                        kdec_all[s * c:(s + 1) * c, 128 * h:128 * (h + 1)]) for s, h in hs]
    outs = []
    for h in range(HG_HEADS):
        sl = slice(128 * h, 128 * (h + 1))
        parts = inter[h * nsb:(h + 1) * nsb]
        o = _dot(a_mats[h], v_all[:, sl]) + (jnp.concatenate(parts, axis=0) if nsb > 1 else parts[0])
        o = o * lax.rsqrt(jnp.mean(o * o, axis=-1, keepdims=True) + NORM_EPS) * gn_ref[...]
        outs.append(o * _silu(g_ref[:, sl]))
    o_ref[...] = jnp.concatenate(outs, axis=1).astype(o_ref.dtype)
    for (s, h), st in zip(hs, st_new):
        if has_init:
            sout_ref[s, h] = st.T
        else:
            st_sc[h] = st

    if not has_init:
        @pl.when(ci == pl.num_programs(1) - 1)
        def _():
            for h in range(HG_HEADS):
                sout_ref[0, h] = st_sc[h].T


def _hgrn(proj, lb_raw, g_norm, s0, layer, nseq, t, c, nsb, valid):
    nc = t // c
    has_init = s0 is not None
    assert (nc == 1) if has_init else (nsb == 1)
    rows = nsb * c
    kern = functools.partial(_hgrn_kernel, layer=layer, c=c, nsb=nsb, valid=valid, has_init=has_init)

    def col(cb):
        return pl.BlockSpec((rows, 1024), lambda b, ci: (b * nc + ci, cb))

    in_specs = [pl.BlockSpec((DEPTH, 1024), lambda b, ci: (0, 0)),
                pl.BlockSpec((1, HG_D), lambda b, ci: (0, 0)),
                col(COL_HGQ), col(COL_HGF), col(COL_HGI), col(COL_GHG)]
    args = [lb_raw, g_norm.reshape(1, HG_D), proj, proj, proj, proj]
    if has_init:
        in_specs.append(pl.BlockSpec((None, nsb, HG_HEADS, HG_D, HG_D), lambda b, ci: (layer, b, 0, 0, 0)))
        args.append(s0)
    return pl.pallas_call(
        kern,
        grid=(nseq // nsb, nc),
        in_specs=in_specs,
        out_specs=[pl.BlockSpec((rows, 1024), lambda b, ci: (b * nc + ci, 0)),
                   pl.BlockSpec((nsb, HG_HEADS, HG_D, HG_D), lambda b, ci: (b, 0, 0, 0))],
        out_shape=[jax.ShapeDtypeStruct((nseq * t, 1024), BF16),
                   jax.ShapeDtypeStruct((nseq, HG_HEADS, HG_D, HG_D), F32)],
        scratch_shapes=[] if has_init else [pltpu.VMEM((HG_HEADS, HG_D, HG_D), F32)],
        compiler_params=_cparams(("parallel", "arbitrary")),
        name="hgrn",
    )(*args)


def _rwkv_kernel(xr_ref, xk_ref, xv_ref, xs_ref, g_ref, mu_ref, w0_ref, wup_ref, a0_ref, aup_ref,
                 kk_ref, ka_ref, rk_ref, gnw_ref, gnb_ref, *rest, c, nsb, valid, has_init):
    if has_init:
        sh0_ref, s0_ref, o_ref, sout_ref = rest
    else:
        o_ref, sout_ref, s_sc, carry_sc = rest
        ci = pl.program_id(1)
    npair = RW_HEADS // 2
    rows = nsb * c
    lc = int(math.log2(c))
    w = RW_WIDTH

    if not has_init:
        @pl.when(ci == 0)
        def _():
            carry_sc[...] = jnp.zeros(carry_sc.shape, F32)
            s_sc[...] = jnp.zeros(s_sc.shape, F32)

    t_col = _iota((rows, 1), 0)
    first = (t_col & (c - 1)) == 0
    if has_init:
        sh0 = jnp.broadcast_to(sh0_ref[...], (nsb, c, RW_SHIFT_PAD)).reshape(rows, RW_SHIFT_PAD)

    def shifted(x, lo, hi):
        before = sh0[:, lo:hi] if has_init else carry_sc[0:1, lo:hi]
        prev = jnp.where(first, before, pltpu.roll(x, 1, 0))
        return x + mu_ref[:, lo:hi] * (prev - x)

    xr, xk, xv = xr_ref[...], xk_ref[...], xv_ref[...]
    xs = xs_ref[...]
    xw = xs[:, 768:896]
    xa = xs[:, 896:1024]
    r = shifted(xr, 0, w)
    k = shifted(xk, w, 2 * w)
    v = shifted(xv, 2 * w, 3 * w)
    wl = shifted(xw, 3 * w, 3 * w + 128)
    al = shifted(xa, 3 * w + 128, 3 * w + 256)
    if not has_init:
        carry_sc[0:1, 0:w] = xr[c - 1:c, :]
        carry_sc[0:1, w:2 * w] = xk[c - 1:c, :]
        carry_sc[0:1, 2 * w:3 * w] = xv[c - 1:c, :]
        carry_sc[0:1, 3 * w:3 * w + 128] = xw[c - 1:c, :]
        carry_sc[0:1, 3 * w + 128:3 * w + 256] = xa[c - 1:c, :]

    zw = w0_ref[...] + _dot(jnp.tanh(wl), wup_ref[...])
    log_w = -math.exp(-0.5) * jax.nn.sigmoid(zw)
    ag = jax.nn.sigmoid(a0_ref[...] + _dot(al, aup_ref[...]))

    seg_b = (_iota((128, 128), 0) >> 6) == (_iota((128, 128), 1) >> 6)
    seg = seg_b.astype(BF16)

    def seg_sum(x):
        return jnp.concatenate(
            [jnp.dot(x[:, 128 * p:128 * (p + 1)].astype(BF16), seg, preferred_element_type=F32)
             for p in range(npair)], axis=1)

    kk = k * kk_ref[...]
    kk = kk * lax.rsqrt(jnp.maximum(seg_sum(kk * kk), 1e-24))
    kf = k * (1.0 + (ag - 1.0) * ka_ref[...])
    if valid < c:
        live = (t_col & (c - 1)) < valid
        kk = jnp.where(live, kk, 0.0)
        kf = jnp.where(live, kf, 0.0)
        log_w = jnp.where(live, log_w, 0.0)

    t_r = _iota((rows, rows), 0)
    t_c = _iota((rows, rows), 1)
    tri = ((t_c <= t_r) & ((t_r >> lc) == (t_c >> lc))).astype(F32)
    eye = (t_c == t_r).astype(F32)
    cum = _dot_split(tri, log_w)
    e_in = jnp.exp(cum)
    e_out = jnp.exp(-cum)
    at_all = -kk * jnp.exp(cum - log_w)
    bt_all = kk * ag * e_out
    kt_all = kf * e_out
    rt_all = r * e_in
    r2 = 2 * rows
    a_r = _iota((r2, r2), 0)
    a_c = _iota((r2, r2), 1)
    tr = a_r & (rows - 1)
    tc = a_c & (rows - 1)
    m4 = ((tr >> lc) == (tc >> lc)) & ((tc < tr) | ((a_r >= rows) & (tc == tr)))
    m4_right = m4 & (a_c >= rows)
    head_lo = _iota((1, 128), 1) < RW_HEAD
    nsteps = 0
    cover = 2
    while cover < valid:
        cover *= 2
        nsteps += 1

    def pair(x, p):
        return x[:, 128 * p:128 * (p + 1)]

    if has_init:
        z64 = jnp.zeros((RW_HEAD, RW_HEAD), F32)
        states = [[jnp.concatenate([jnp.concatenate([s0_ref[s, 2 * p], z64], axis=1),
                                    jnp.concatenate([z64, s0_ref[s, 2 * p + 1]], axis=1)], axis=0)
                   for p in range(npair)] for s in range(nsb)]
    else:
        states = [[s_sc[p] for p in range(npair)]]

    ar_at, ar_rt = [], []
    for p in range(npair):
        at_p, rt_p = pair(at_all, p), pair(rt_all, p)
        pa, pr = [], []
        for s in range(nsb):
            rs = slice(s * c, (s + 1) * c)
            ar = _dot_nt(jnp.concatenate([at_p[rs], rt_p[rs]], axis=0), states[s][p])
            pa.append(ar[0:c])
            pr.append(ar[c:2 * c])
        ar_at.append(jnp.concatenate(pa, axis=0) if nsb > 1 else pa[0])
        ar_rt.append(jnp.concatenate(pr, axis=0) if nsb > 1 else pr[0])

    heads = [(p, hh) for p in range(npair) for hh in range(2)]
    hmask = [head_lo if hh == 0 else jnp.logical_not(head_lo) for _, hh in heads]
    labs, mrbs, rhss, ykv = [], [], [], []
    for (p, hh), hm in zip(heads, hmask):
        lhs = jnp.concatenate([jnp.where(hm, pair(at_all, p), 0.0), jnp.where(hm, pair(rt_all, p), 0.0)], axis=0)
        rhs_t = jnp.concatenate([pair(bt_all, p), pair(kt_all, p)], axis=0)
        prod = _dot_nt(lhs, rhs_t)
        vp = pair(v, p)
        lkv = _dot(jnp.where(m4_right, prod, 0.0), jnp.concatenate([vp, vp], axis=0))
        m4p = jnp.where(m4, prod, 0.0)
        labs.append(m4p[0:rows, 0:rows])
        mrbs.append(m4p[rows:r2, 0:rows])
        rhss.append(jnp.where(hm, ar_at[p] + lkv[0:rows], 0.0))
        ykv.append(jnp.where(hm, lkv[rows:r2], 0.0))

    xs_ = [eye + lab for lab in labs]
    pws = labs
    for _ in range(nsteps):
        pws = [_dot(pw, pw) for pw in pws]
        xs_ = [x + _dot(x, pw) for x, pw in zip(xs_, pws)]

    us = [_dot(x, rhs) for x, rhs in zip(xs_, rhss)]
    ys = [_dot(mrb, u) + yk for mrb, u, yk in zip(mrbs, us, ykv)]
    u_pair = [us[2 * p] + us[2 * p + 1] for p in range(npair)]
    y = jnp.concatenate([ar_rt[p] + ys[2 * p] + ys[2 * p + 1] for p in range(npair)], axis=1)

    for p in range(npair):
        bt_p, kt_p, vp, up = pair(bt_all, p), pair(kt_all, p), pair(v, p), u_pair[p]
        for s in range(nsb):
            rs = slice(s * c, (s + 1) * c)
            pc = pair(e_in, p)[s * c + c - 1:s * c + c, :]
            s_new = states[s][p] * pc + _dot_tn(jnp.concatenate([up[rs], vp[rs]], axis=0),
                                                jnp.concatenate([bt_p[rs] * pc, kt_p[rs] * pc], axis=0))
            s_new = jnp.where(seg_b, s_new, 0.0)
            if has_init:
                sout_ref[s, 2 * p] = s_new[0:RW_HEAD, 0:RW_HEAD]
                sout_ref[s, 2 * p + 1] = s_new[RW_HEAD:128, RW_HEAD:128]
            else:
                s_sc[p] = s_new

    mean = seg_sum(y) * (1.0 / RW_HEAD)
    d = y - mean
    var = seg_sum(d * d) * (1.0 / RW_HEAD)
    yn = d * lax.rsqrt(var + GN_EPS) * gnw_ref[...] + gnb_ref[...]
    yn = yn + seg_sum(r * kf * rk_ref[...]) * v
    o_ref[...] = (yn * _silu(g_ref[...])).astype(o_ref.dtype)

    if not has_init:
        @pl.when(ci == pl.num_programs(1) - 1)
        def _():
            for p in range(npair):
                s_bd = s_sc[p]
                sout_ref[0, 2 * p] = s_bd[0:RW_HEAD, 0:RW_HEAD]
                sout_ref[0, 2 * p + 1] = s_bd[RW_HEAD:128, RW_HEAD:128]


def _rwkv(proj, small, prm, sh0, s0, nseq, t, c, nsb, valid):
    nc = t // c
    has_init = s0 is not None
    assert (nc == 1) if has_init else (nsb == 1)
    rows = nsb * c
    kern = functools.partial(_rwkv_kernel, c=c, nsb=nsb, valid=valid, has_init=has_init)

    def col(cb):
        return pl.BlockSpec((rows, 1024), lambda b, ci: (b * nc + ci, cb))

    def const(shape):
        return pl.BlockSpec(shape, lambda b, ci: (0,) * len(shape))

    in_specs = [col(COL_RWR), col(COL_RWK), col(COL_RWV), col(0), col(COL_GRW),
                const((1, RW_SHIFT_PAD)), const((1, 1024)), const((128, 1024)), const((1, 1024)),
                const((128, 1024)), const((1, 1024)), const((1, 1024)), const((1, 1024)),
                const((1, 1024)), const((1, 1024))]
    args = [proj, proj, proj, small, proj, prm["mu"], prm["w0"], prm["w_up"], prm["a0"], prm["a_up"],
            prm["k_k"], prm["k_a"], prm["r_k"], prm["gn_w"], prm["gn_b"]]
    scratch = []
    if has_init:
        in_specs += [pl.BlockSpec((nsb, 1, RW_SHIFT_PAD), lambda b, ci: (b, 0, 0)),
                     pl.BlockSpec((nsb, RW_HEADS, RW_HEAD, RW_HEAD), lambda b, ci: (b, 0, 0, 0))]
        args += [sh0, s0]
    else:
        scratch = [pltpu.VMEM((RW_HEADS // 2, 128, 128), F32), pltpu.VMEM((8, RW_SHIFT_PAD), F32)]
    return pl.pallas_call(
        kern,
        grid=(nseq // nsb, nc),
        in_specs=in_specs,
        out_specs=[pl.BlockSpec((rows, 1024), lambda b, ci: (b * nc + ci, 0)),
                   pl.BlockSpec((nsb, RW_HEADS, RW_HEAD, RW_HEAD), lambda b, ci: (b, 0, 0, 0))],
        out_shape=[jax.ShapeDtypeStruct((nseq * t, 1024), BF16),
                   jax.ShapeDtypeStruct((nseq, RW_HEADS, RW_HEAD, RW_HEAD), F32)],
        scratch_shapes=scratch,
        compiler_params=_cparams(("parallel", "arbitrary")),
        name="rwkv",
    )(*args)


def _out_proj_kernel(x_ref, m_ref, hg_ref, rw_ref, w_ref, g_ref, o_ref):
    acc = _dot(m_ref[...], w_ref[0:2048, :])
    acc = acc + _dot(hg_ref[...], w_ref[2048:3072, :])
    acc = acc + _dot(rw_ref[...], w_ref[3072:4096, :])
    y = acc * lax.rsqrt(jnp.mean(acc * acc, axis=-1, keepdims=True) + NORM_EPS) * g_ref[...]
    o_ref[...] = x_ref[...] + y


def _out_proj(x, m_mla, m_hg, m_rw, w_out, post_norm, tm=256):
    n = x.shape[0]
    return pl.pallas_call(
        _out_proj_kernel,
        grid=(n // tm,),
        in_specs=[pl.BlockSpec((tm, 2048), lambda i: (i, 0)),
                  pl.BlockSpec((tm, 2048), lambda i: (i, 0)),
                  pl.BlockSpec((tm, 1024), lambda i: (i, 0)),
                  pl.BlockSpec((tm, 1024), lambda i: (i, 0)),
                  pl.BlockSpec((4096, 2048), lambda i: (0, 0), pipeline_mode=pl.Buffered(1)),
                  pl.BlockSpec((1, 2048), lambda i: (0, 0))],
        out_specs=pl.BlockSpec((tm, 2048), lambda i: (i, 0)),
        out_shape=jax.ShapeDtypeStruct((n, 2048), F32),
        compiler_params=_cparams(("parallel",)),
        name="out_proj",
    )(x, m_mla, m_hg, m_rw, w_out, post_norm.reshape(1, -1))


def _prep_w_small(w_in_t):
    kr = w_in_t[:, 768:832]
    kr_sw = jnp.concatenate([-kr[:, 32:], kr[:, :32]], axis=1)
    wl = w_in_t[:, RW_X_ROW + 3072:RW_X_ROW + 3136]
    al = w_in_t[:, RW_X_ROW + 3136:RW_X_ROW + 3200]
    return jnp.concatenate([w_in_t[:, 0:832], wl, kr_sw, al], axis=1)


def _prep_wq(w_q_up):
    d, r, h, _ = w_q_up.shape
    nope = w_q_up[..., :MLA_NOPE].reshape(d, r, h * MLA_NOPE)
    pe = w_q_up[..., MLA_NOPE:]
    pe_sw = jnp.concatenate([-pe[..., 32:], pe[..., :32]], axis=-1)
    pad = lambda a: jnp.pad(a, ((0, 0), (0, 0), (0, 0), (0, 64))).reshape(d, r, h * 128)
    return jnp.concatenate([nope, pad(pe), pad(pe_sw)], axis=-1).astype(BF16)


def _reorder_shift(v):
    z = jnp.zeros(v.shape[:-1] + (64,), v.dtype)
    return jnp.concatenate([v[..., :3072], z, v[..., 3072:3136], z, v[..., 3136:3200]], axis=-1)


def kernel(x_prompt, x_sample, cache_mla, page_table, state_hgrn, state_rwkv, state_rwkv_shift, pre_norm, post_norm, w_in, w_out, mla_q_norm, mla_kv_norm, mla_w_q_up, mla_w_kv_up, hg_lower_bound, hg_g_norm, rw_mu, rw_w0, rw_w_up, rw_a0, rw_a_up, rw_k_k, rw_k_a, rw_r_k, rw_gn_w, rw_gn_b):
    nb, t, d = x_prompt.shape
    db, dt, _ = x_sample.shape
    n_pages = page_table.shape[1]
    past_len = n_pages * PAGE
    sp = SAMPLE_PAD

    w_in_t = jnp.swapaxes(w_in, 1, 2)
    w_small_t = _prep_w_small(w_in_t)
    wq_r = _prep_wq(mla_w_q_up)
    wuk = jnp.transpose(mla_w_kv_up[..., :MLA_NOPE], (0, 2, 3, 1)).astype(BF16)
    wuv = jnp.transpose(mla_w_kv_up[..., MLA_NOPE:], (0, 2, 1, 3)).astype(BF16)
    w_out_b = w_out.astype(BF16)
    zpad = jnp.zeros((DEPTH, 64, RW_WIDTH), F32)
    w_up_p = jnp.concatenate([zpad, rw_w_up], axis=1).astype(BF16)
    a_up_p = jnp.concatenate([zpad, rw_a_up], axis=1).astype(BF16)
    mu_r = _reorder_shift(rw_mu)
    sh0_r = _reorder_shift(state_rwkv_shift).reshape(DEPTH, db, 1, RW_SHIFT_PAD)
    half = MLA_ROPE // 2
    inv = ROPE_BASE ** (-jnp.arange(half, dtype=F32) / half)
    inv128 = jnp.tile(inv, 4).reshape(1, 128)
    pos_p = jnp.tile(jnp.arange(t, dtype=F32), nb).reshape(nb * t, 1)
    pos_s = jnp.tile(past_len + jnp.arange(sp, dtype=F32), db).reshape(db * sp, 1)
    pt_flat = page_table.reshape(-1)
    cache_t = jnp.swapaxes(cache_mla, 2, 3)

    xp = x_prompt.reshape(nb * t, d)
    xs = jnp.pad(x_sample, ((0, 0), (0, sp - dt), (0, 0))).reshape(db * sp, d)

    rows_p, rows_s, hg_p, hg_s, rw_p, rw_s, sh_p, sh_s = [], [], [], [], [], [], [], []
    for l in range(DEPTH):
        prm = {"mu": mu_r[l:l + 1], "w0": rw_w0[l:l + 1], "w_up": w_up_p[l], "a0": rw_a0[l:l + 1],
               "a_up": a_up_p[l], "k_k": rw_k_k[l:l + 1], "k_a": rw_k_a[l:l + 1],
               "r_k": rw_r_k[l].reshape(1, RW_WIDTH), "gn_w": rw_gn_w[l:l + 1], "gn_b": rw_gn_b[l:l + 1]}

        h = _rms_cast(xp, pre_norm[l])
        small = _in_proj(h, w_small_t, l, (0,))
        proj = _in_proj(h, w_in_t, l, MAIN_W_ROWS)
        qc, rows, kc = _mla_prep(small, pos_p, inv128, mla_q_norm[l], mla_kv_norm[l], wq_r[l], wuk[l])
        m_mla = _attn_prompt(qc, kc, proj, wuv[l], nb, t)
        m_hg, hg_state = _hgrn(proj, hg_lower_bound, hg_g_norm[l], None, l, nb, t, 128, 1, 128)
        m_rw, rw_state = _rwkv(proj, small, prm, None, None, nb, t, 128, 1, 128)
        rw_off = COL_RWR * 1024
        sh = jnp.concatenate([proj.reshape(nb, t, D_MAIN)[:, t - 1, rw_off:rw_off + 3072],
                              small.reshape(nb, t, 1024)[:, t - 1, 832:896],
                              small.reshape(nb, t, 1024)[:, t - 1, 960:1024]], axis=-1)
        xp = _out_proj(xp, m_mla, m_hg, m_rw, w_out_b[l], post_norm[l])
        rows_p.append(rows.reshape(nb, t, MLA_CACHE)); hg_p.append(hg_state); rw_p.append(rw_state); sh_p.append(sh)

        h = _rms_cast(xs, pre_norm[l])
        small = _in_proj(h, w_small_t, l, (0,))
        proj = _in_proj(h, w_in_t, l, MAIN_W_ROWS)
        qc, rows, kc = _mla_prep(small, pos_s, inv128, mla_q_norm[l], mla_kv_norm[l], wq_r[l], wuk[l])
        q_s = qc.reshape(MLA_HEADS, db, sp, MLA_QK)[:, :, :dt]
        q_s = jnp.transpose(q_s, (1, 0, 2, 3)).reshape(db, MLA_HEADS * dt, MLA_QK)
        newk = jnp.pad(kc.reshape(db, sp, MLA_QK)[:, :dt], ((0, 0), (0, 16 - dt), (0, 0)))
        o_lat = _attn_sample(pt_flat, q_s, newk, cache_t, l, n_pages, dt)
        o_lat = jnp.transpose(o_lat.reshape(db, MLA_HEADS, dt, MLA_KVR), (1, 0, 2, 3))
        o_lat = jnp.pad(o_lat, ((0, 0), (0, 0), (0, sp - dt), (0, 0))).reshape(MLA_HEADS, db * sp, MLA_KVR)
        m_mla = _oproj_gate(o_lat, proj, wuv[l])
        m_hg, hg_state = _hgrn(proj, hg_lower_bound, hg_g_norm[l], state_hgrn, l, db, sp, sp, 8, dt)
        m_rw, rw_state = _rwkv(proj, small, prm, sh0_r[l], state_rwkv[l], db, sp, sp, 8, dt)
        pr = proj.reshape(db, sp, D_MAIN)[:, dt - 1]
        ps = small.reshape(db, sp, 1024)[:, dt - 1]
        sh = jnp.concatenate([pr[:, rw_off:rw_off + 3072], ps[:, 832:896], ps[:, 960:1024]], axis=-1)
        xs = _out_proj(xs, m_mla, m_hg, m_rw, w_out_b[l], post_norm[l])
        rows_s.append(rows.reshape(db, sp, MLA_CACHE)[:, :dt]); hg_s.append(hg_state); rw_s.append(rw_state); sh_s.append(sh)

    y_p = xp.reshape(nb, t, d)
    y_s = xs.reshape(db, sp, d)[:, :dt]
    return (y_p, y_s, jnp.stack(rows_p), jnp.stack(rows_s), jnp.stack(hg_p), jnp.stack(hg_s),
            jnp.stack(rw_p), jnp.stack(rw_s), jnp.stack(sh_p), jnp.stack(sh_s))
```

```python
import functools
import math

import jax
import jax.numpy as jnp
import numpy as np
from jax import lax
from jax.experimental import pallas as pl
from jax.experimental.pallas import tpu as pltpu

F32 = jnp.float32
BF16 = jnp.bfloat16

D_MODEL = 2048
DEPTH = 4
PAGE = 128
MLA_HEADS = 16
MLA_NOPE = 128
MLA_ROPE = 64
MLA_V = 128
MLA_QR = 512
MLA_KVR = 256
MLA_CACHE = MLA_KVR + MLA_ROPE
MLA_QK = MLA_KVR + 128
MLA_SCALE = 1.0 / math.sqrt(MLA_NOPE + MLA_ROPE)
ROPE_BASE = 10000.0
NEG_BIG = -1e30
HG_HEADS = 8
HG_D = 128
HG_F_MIN = 1e-30
RW_HEADS = 16
RW_HEAD = 64
RW_WIDTH = 1024
RW_LORA = 64
RW_SHIFT = 3 * RW_WIDTH + 2 * RW_LORA
RW_SHIFT_PAD = 3 * RW_WIDTH + 256
GN_EPS = 64e-5
NORM_EPS = 1e-6
SAMPLE_PAD = 8
VMEM_LIMIT = 48 * 1024 * 1024

COL_GMLA, COL_HGQ, COL_HGF, COL_HGI, COL_GHG, COL_GRW, COL_RWR, COL_RWK, COL_RWV = (0, 2, 3, 4, 5, 6, 7, 8, 9)
MAIN_W_ROWS = (832, 1856, 2880, 3904, 4928, 5952, 10176, 6976, 8000, 9024)
D_MAIN = 10 * 1024
RW_X_ROW = 6976


def _cparams(sem):
    return pltpu.CompilerParams(dimension_semantics=sem, vmem_limit_bytes=VMEM_LIMIT)


def _dot(a, b):
    return jnp.dot(a.astype(BF16), b.astype(BF16), preferred_element_type=F32)


def _dot_nt(a, b):
    return lax.dot_general(a.astype(BF16), b.astype(BF16), (((1,), (1,)), ((), ())),
                           preferred_element_type=F32)


def _dot_tn(a, b):
    return lax.dot_general(a.astype(BF16), b.astype(BF16), (((0,), (0,)), ((), ())),
                           preferred_element_type=F32)


def _dot_split(m01, x):
    m = m01.astype(BF16)
    hi = x.astype(BF16)
    r1 = x - hi.astype(F32)
    mid = r1.astype(BF16)
    lo = (r1 - mid.astype(F32)).astype(BF16)
    d = lambda p: jnp.dot(m, p, preferred_element_type=F32)
    return d(hi) + d(mid) + d(lo)


def _dot_split_r(x, m01):
    m = m01.astype(BF16)
    hi = x.astype(BF16)
    r1 = x - hi.astype(F32)
    mid = r1.astype(BF16)
    lo = (r1 - mid.astype(F32)).astype(BF16)
    d = lambda p: jnp.dot(p, m, preferred_element_type=F32)
    return d(hi) + d(mid) + d(lo)


def _silu(x):
    return x * jax.nn.sigmoid(x)


def _iota(shape, dim):
    return lax.broadcasted_iota(jnp.int32, shape, dim)


def _rms_cast_kernel(x_ref, g_ref, o_ref):
    x = x_ref[...]
    y = x * lax.rsqrt(jnp.mean(x * x, axis=-1, keepdims=True) + NORM_EPS)
    o_ref[...] = (y * g_ref[...]).astype(BF16)


def _rms_cast(x, g, tm=512):
    n, d = x.shape
    return pl.pallas_call(
        _rms_cast_kernel,
        grid=(n // tm,),
        in_specs=[pl.BlockSpec((tm, d), lambda i: (i, 0)),
                  pl.BlockSpec((1, d), lambda i: (0, 0))],
        out_specs=pl.BlockSpec((tm, d), lambda i: (i, 0)),
        out_shape=jax.ShapeDtypeStruct((n, d), BF16),
        compiler_params=_cparams(("parallel",)),
        name="rms_cast",
    )(x, g.reshape(1, d))


def _in_proj_kernel(rows_ref, a_ref, w_ref, o_ref, wb_sc):
    @pl.when(pl.program_id(1) == 0)
    def _():
        wb_sc[...] = w_ref[0].astype(BF16)

    o_ref[...] = lax.dot_general(a_ref[...], wb_sc[...], (((1,), (1,)), ((), ())),
                                 preferred_element_type=F32)


def _in_proj(a, w_t, layer, w_rows, tm=512, tn=1024):
    m, k = a.shape
    nblk = len(w_rows)
    grid_spec = pltpu.PrefetchScalarGridSpec(
        num_scalar_prefetch=1,
        grid=(nblk, m // tm),
        in_specs=[pl.BlockSpec((tm, k), lambda j, i, r: (i, 0)),
                  pl.BlockSpec((pl.Element(1), pl.Element(tn), pl.Element(k)),
                               lambda j, i, r: (layer, r[j] * 64, 0))],
        out_specs=pl.BlockSpec((tm, tn), lambda j, i, r: (i, j)),
        scratch_shapes=[pltpu.VMEM((tn, k), BF16)],
    )
    return pl.pallas_call(
        _in_proj_kernel,
        grid_spec=grid_spec,
        out_shape=jax.ShapeDtypeStruct((m, nblk * tn), F32),
        compiler_params=_cparams(("arbitrary", "arbitrary")),
        name="in_proj",
    )(jnp.asarray([r // 64 for r in w_rows], jnp.int32), a, w_t)


def _mla_prep_kernel(p_ref, pos_ref, inv_ref, qn_ref, kvn_ref, wq_ref, wuk_ref,
                     qc_ref, rows_ref, kc_ref):
    x = p_ref[...]
    qd = x[:, 0:MLA_QR]
    kvd = x[:, MLA_QR:MLA_QR + MLA_KVR]
    kr = x[:, 768:832]
    krsw = x[:, 896:960]
    qn = qd * lax.rsqrt(jnp.mean(qd * qd, axis=-1, keepdims=True) + NORM_EPS) * qn_ref[...]
    q = _dot(qn, wq_ref[...])
    ang = pos_ref[...] * inv_ref[...]
    cos = jnp.cos(ang)
    sin = jnp.sin(ang)
    hw = MLA_HEADS * 128
    for h in range(MLA_HEADS):
        ql = _dot(q[:, 128 * h:128 * (h + 1)], wuk_ref[h])
        pe = (q[:, hw + 128 * h:hw + 128 * (h + 1)] * cos
              + q[:, 2 * hw + 128 * h:2 * hw + 128 * (h + 1)] * sin)
        qc_ref[h, :, 0:MLA_KVR] = (ql * MLA_SCALE).astype(BF16)
        qc_ref[h, :, MLA_KVR:MLA_QK] = (pe * MLA_SCALE).astype(BF16)
    ckv = kvd * lax.rsqrt(jnp.mean(kvd * kvd, axis=-1, keepdims=True) + NORM_EPS) * kvn_ref[...]
    kpe = kr * cos[:, 0:64] + krsw * sin[:, 0:64]
    rows_ref[:, 0:MLA_KVR] = ckv
    rows_ref[:, MLA_KVR:MLA_CACHE] = kpe
    kc_ref[:, 0:MLA_KVR] = ckv.astype(BF16)
    kc_ref[:, MLA_KVR:MLA_CACHE] = kpe.astype(BF16)
    kc_ref[:, MLA_CACHE:MLA_QK] = jnp.zeros((x.shape[0], MLA_QK - MLA_CACHE), BF16)


def _mla_prep(proj, pos, inv128, q_norm, kv_norm, wq, wuk, tq=256):
    n = proj.shape[0]
    return pl.pallas_call(
        _mla_prep_kernel,
        grid=(n // tq,),
        in_specs=[pl.BlockSpec((tq, 1024), lambda i: (i, 0)),
                  pl.BlockSpec((tq, 1), lambda i: (i, 0)),
                  pl.BlockSpec((1, 128), lambda i: (0, 0)),
                  pl.BlockSpec((1, MLA_QR), lambda i: (0, 0)),
                  pl.BlockSpec((1, MLA_KVR), lambda i: (0, 0)),
                  pl.BlockSpec(wq.shape, lambda i: (0, 0)),
                  pl.BlockSpec(wuk.shape, lambda i: (0, 0, 0))],
        out_specs=[pl.BlockSpec((MLA_HEADS, tq, MLA_QK), lambda i: (0, i, 0)),
                   pl.BlockSpec((tq, MLA_CACHE), lambda i: (i, 0)),
                   pl.BlockSpec((tq, MLA_QK), lambda i: (i, 0))],
        out_shape=[jax.ShapeDtypeStruct((MLA_HEADS, n, MLA_QK), BF16),
                   jax.ShapeDtypeStruct((n, MLA_CACHE), F32),
                   jax.ShapeDtypeStruct((n, MLA_QK), BF16)],
        compiler_params=_cparams(("parallel",)),
        name="mla_prep",
    )(proj, pos, inv128, q_norm.reshape(1, -1), kv_norm.reshape(1, -1), wq, wuk)


def _attn_prompt_kernel(q_ref, k_ref, g_ref, wuv_ref, o_ref, m_sc, l_sc, acc_sc, *, tq, tk, hg):
    i = pl.program_id(1)
    j = pl.program_id(2)
    last = (i * tq + tq - 1) // tk
    rows = MLA_HEADS * tq

    @pl.when(j == 0)
    def _():
        m_sc[...] = jnp.full(m_sc.shape, NEG_BIG, F32)
        l_sc[...] = jnp.zeros(l_sc.shape, F32)
        acc_sc[...] = jnp.zeros(acc_sc.shape, F32)

    def step(masked):
        k = k_ref[...]
        kv = k[:, 0:MLA_KVR]
        grows = hg * tq
        groups = range(MLA_HEADS // hg)
        rsl = [slice(g * grows, (g + 1) * grows) for g in groups]
        m_prev = [m_sc[rs, :] for rs in rsl]
        l_prev = [l_sc[rs, :] for rs in rsl]
        a_prev = [acc_sc[rs, :] for rs in rsl]
        ss = [lax.dot_general(q_ref[g * hg:(g + 1) * hg].reshape(grows, MLA_QK), k,
                              (((1,), (1,)), ((), ())), preferred_element_type=F32) for g in groups]
        if masked:
            qpos = i * tq + (_iota((grows, 1), 0) & (tq - 1))
            kpos = j * tk + _iota((1, tk), 1)
            keep = kpos <= qpos
            ss = [jnp.where(keep, s, NEG_BIG) for s in ss]
        m_new = [jnp.maximum(m_prev[g], jnp.max(ss[g], axis=-1, keepdims=True)) for g in groups]
        alpha = [jnp.exp(m_prev[g] - m_new[g]) for g in groups]
        ps = [jnp.exp(ss[g] - m_new[g]) for g in groups]
        pv = [jnp.dot(ps[g].astype(BF16), kv, preferred_element_type=F32) for g in groups]
        for g, rs in zip(groups, rsl):
            l_sc[rs, :] = alpha[g] * l_prev[g] + jnp.sum(ps[g], axis=-1, keepdims=True)
            acc_sc[rs, :] = alpha[g] * a_prev[g] + pv[g]
            m_sc[rs, :] = m_new[g]

    @pl.when(j < last)
    def _():
        step(False)

    @pl.when(j == last)
    def _():
        step(True)

    @pl.when(j == last)
    def _():
        o_lat = acc_sc[...] / l_sc[...]
        g = g_ref[...]
        for h in range(MLA_HEADS):
            oh = _dot(o_lat[h * tq:(h + 1) * tq, :], wuv_ref[h])
            o_ref[:, 128 * h:128 * (h + 1)] = (oh * _silu(g[:, 128 * h:128 * (h + 1)])).astype(BF16)


def _attn_prompt(qc, kc, proj, wuv, nb, t, tq=256, tk=512, hg=1):
    nq = t // tq
    nk = t // tk
    assert tk % tq == 0 and MLA_HEADS % hg == 0
    kern = functools.partial(_attn_prompt_kernel, tq=tq, tk=tk, hg=hg)

    def kmap(b, i, j):
        return (b * nk + jnp.minimum(j, (i * tq + tq - 1) // tk), 0)

    return pl.pallas_call(
        kern,
        grid=(nb, nq, nk),
        in_specs=[pl.BlockSpec((MLA_HEADS, tq, MLA_QK), lambda b, i, j: (0, b * nq + i, 0)),
                  pl.BlockSpec((tk, MLA_QK), kmap),
                  pl.BlockSpec((tq, 2048), lambda b, i, j: (b * nq + i, COL_GMLA // 2)),
                  pl.BlockSpec(wuv.shape, lambda b, i, j: (0, 0, 0))],
        out_specs=pl.BlockSpec((tq, 2048), lambda b, i, j: (b * nq + i, 0)),
        out_shape=jax.ShapeDtypeStruct((nb * t, 2048), BF16),
        scratch_shapes=[pltpu.VMEM((MLA_HEADS * tq, 1), F32),
                        pltpu.VMEM((MLA_HEADS * tq, 1), F32),
                        pltpu.VMEM((MLA_HEADS * tq, MLA_KVR), F32)],
        compiler_params=_cparams(("parallel", "parallel", "arbitrary")),
        name="attn_prompt",
    )(qc, kc, proj, wuv)


def _attn_sample_kernel(pt_ref, q_ref, nk_ref, *rest, layer, nseq, npg, dt):
    pages = rest[:nseq * npg]
    o_ref = rest[nseq * npg]
    m_sc, l_sc, acc_sc = rest[nseq * npg + 1:]
    j = pl.program_id(1)
    rows = MLA_HEADS * dt

    @pl.when(j == 0)
    def _():
        for s in range(nseq):
            q = q_ref[s]
            kn = nk_ref[s]
            sc = lax.dot_general(q, kn, (((1,), (1,)), ((), ())), preferred_element_type=F32)
            trow = _iota((rows, 1), 0) & (dt - 1)
            kcol = _iota((1, kn.shape[0]), 1)
            sc = jnp.where(kcol <= trow, sc, NEG_BIG)
            m = jnp.max(sc, axis=-1, keepdims=True)
            p = jnp.exp(sc - m)
            m_sc[s] = m
            l_sc[s] = jnp.sum(p, axis=-1, keepdims=True)
            acc_sc[s] = jnp.dot(p.astype(BF16), kn[:, 0:MLA_KVR], preferred_element_type=F32)

    seqs = range(nseq)
    kts = [jnp.concatenate([pages[s * npg + g][...].astype(BF16) for g in range(npg)], axis=1) for s in seqs]
    scs = [jnp.dot(q_ref[s][:, 0:MLA_CACHE], kts[s], preferred_element_type=F32) for s in seqs]
    m_prev = [m_sc[s] for s in seqs]
    l_prev = [l_sc[s] for s in seqs]
    a_prev = [acc_sc[s] for s in seqs]
    m_new = [jnp.maximum(m_prev[s], jnp.max(scs[s], axis=-1, keepdims=True)) for s in seqs]
    alpha = [jnp.exp(m_prev[s] - m_new[s]) for s in seqs]
    ps = [jnp.exp(scs[s] - m_new[s]) for s in seqs]
    pv = [lax.dot_general(ps[s].astype(BF16), kts[s][0:MLA_KVR, :], (((1,), (1,)), ((), ())),
                          preferred_element_type=F32) for s in seqs]
    for s in seqs:
        l_sc[s] = alpha[s] * l_prev[s] + jnp.sum(ps[s], axis=-1, keepdims=True)
        acc_sc[s] = alpha[s] * a_prev[s] + pv[s]
        m_sc[s] = m_new[s]

    @pl.when(j == pl.num_programs(1) - 1)
    def _():
        for s in range(nseq):
            o_ref[s] = acc_sc[s] / l_sc[s]


def _attn_sample(page_table_flat, q_s, newk, cache, layer, n_pages, dt, nseq=4, npg=8):
    db = q_s.shape[0]
    rows = MLA_HEADS * dt
    ngrp = n_pages // npg
    kern = functools.partial(_attn_sample_kernel, layer=layer, nseq=nseq, npg=npg, dt=dt)

    def page_spec(s, g):
        def imap(bb, j, pt):
            return (layer, pt[(bb * nseq + s) * n_pages + j * npg + g], 0, 0)
        return pl.BlockSpec((None, None, MLA_CACHE, PAGE), imap)

    in_specs = [pl.BlockSpec((nseq, rows, MLA_QK), lambda bb, j, pt: (bb, 0, 0)),
                pl.BlockSpec((nseq, 16, MLA_QK), lambda bb, j, pt: (bb, 0, 0))]
    in_specs += [page_spec(s, g) for s in range(nseq) for g in range(npg)]
    grid_spec = pltpu.PrefetchScalarGridSpec(
        num_scalar_prefetch=1,
        grid=(db // nseq, ngrp),
        in_specs=in_specs,
        out_specs=pl.BlockSpec((nseq, rows, MLA_KVR), lambda bb, j, pt: (bb, 0, 0)),
        scratch_shapes=[pltpu.VMEM((nseq, rows, 1), F32),
                        pltpu.VMEM((nseq, rows, 1), F32),
                        pltpu.VMEM((nseq, rows, MLA_KVR), F32)],
    )
    return pl.pallas_call(
        kern,
        grid_spec=grid_spec,
        out_shape=jax.ShapeDtypeStruct((db, rows, MLA_KVR), F32),
        compiler_params=_cparams(("parallel", "arbitrary")),
        name="attn_sample",
    )(page_table_flat, q_s, newk, *([cache] * (nseq * npg)))


def _oproj_gate_kernel(o_ref, g_ref, wuv_ref, out_ref):
    g = g_ref[...]
    for h in range(MLA_HEADS):
        oh = _dot(o_ref[h], wuv_ref[h])
        out_ref[:, 128 * h:128 * (h + 1)] = (oh * _silu(g[:, 128 * h:128 * (h + 1)])).astype(BF16)


def _oproj_gate(o_lat, proj, wuv, tm=256):
    n = o_lat.shape[1]
    return pl.pallas_call(
        _oproj_gate_kernel,
        grid=(n // tm,),
        in_specs=[pl.BlockSpec((MLA_HEADS, tm, MLA_KVR), lambda i: (0, i, 0)),
                  pl.BlockSpec((tm, 2048), lambda i: (i, COL_GMLA // 2)),
                  pl.BlockSpec(wuv.shape, lambda i: (0, 0, 0))],
        out_specs=pl.BlockSpec((tm, 2048), lambda i: (i, 0)),
        out_shape=jax.ShapeDtypeStruct((n, 2048), BF16),
        compiler_params=_cparams(("parallel",)),
        name="oproj_gate",
    )(o_lat, proj, wuv)


def _hgrn_kernel(lb_ref, gn_ref, q_ref, f_ref, i_ref, g_ref, *rest, layer, c, nsb, valid, has_init):
    if has_init:
        s0_ref, o_ref, sout_ref = rest
    else:
        o_ref, sout_ref, st_sc = rest
        ci = pl.program_id(1)

        @pl.when(ci == 0)
        def _():
            st_sc[...] = jnp.zeros(st_sc.shape, F32)
    rows = nsb * c
    lc = int(math.log2(c))

    lbx = lb_ref[...]
    e = jnp.exp(lbx - jnp.max(lbx, axis=0, keepdims=True))
    sm = e / jnp.sum(e, axis=0, keepdims=True)
    lb = jnp.zeros((1, lbx.shape[1]), F32)
    for l in range(1, layer + 1):
        lb = lb + sm[l:l + 1, :]

    z = f_ref[...]
    q_all = _silu(q_ref[...])
    f = lb + (1.0 - lb) * jax.nn.sigmoid(z)
    logf_all = jnp.log(jnp.maximum(f, HG_F_MIN))
    k_all = (1.0 - lb) * jax.nn.sigmoid(-z)
    v_all = i_ref[...]
    t_col = _iota((rows, 1), 0)
    if valid < c:
        live = (t_col & (c - 1)) < valid
        logf_all = jnp.where(live, logf_all, 0.0)
        k_all = jnp.where(live, k_all, 0.0)

    t_r = _iota((rows, rows), 0)
    t_c = _iota((rows, rows), 1)
    tri = ((t_c <= t_r) & ((t_r >> lc) == (t_c >> lc))).astype(F32)
    m4 = t_r & 3
    lvl2 = (((m4 == 2) & (t_c == t_r)) | ((m4 == 3) & ((t_c == t_r) | (t_c == t_r - 1)))
            | ((m4 == 0) & (t_c == t_r + 1))).astype(F32)
    cum_all = _dot_split(tri, logf_all)
    e2_all = _dot_split(lvl2, logf_all) if c >= 4 else None
    last_all = jnp.broadcast_to(cum_all.reshape(nsb, c, 1024)[:, c - 1:c, :], (nsb, c, 1024)).reshape(rows, 1024)
    kdec_all = k_all * jnp.exp(last_all - cum_all)
    qdec_all = q_all * jnp.exp(cum_all)

    if has_init:
        states = [[s0_ref[s, h].T for h in range(HG_HEADS)] for s in range(nsb)]
    else:
        states = [[st_sc[h] for h in range(HG_HEADS)]]

    a_mats = []
    for h in range(HG_HEADS):
        sl = slice(128 * h, 128 * (h + 1))
        q, k, logf, cum = q_all[:, sl], k_all[:, sl], logf_all[:, sl], cum_all[:, sl]
        a = jnp.zeros((rows, rows), F32)
        half = c // 2
        while half >= 1:
            upper = (t_col & (2 * half - 1)) >= half
            if half >= 4:
                nblk = rows // (2 * half)
                c3 = cum.reshape(nblk, 2 * half, 128)
                ref = jnp.broadcast_to(c3[:, half - 1:half, :], (nblk, 2 * half, 128)).reshape(rows, 128)
                eq = jnp.where(upper, cum - ref, NEG_BIG)
                ek = jnp.where(upper, NEG_BIG, ref - cum)
            elif half == 2:
                e2 = e2_all[:, sl]
                eq = jnp.where(upper, e2, NEG_BIG)
                ek = jnp.where(upper, NEG_BIG, e2)
            else:
                eq = jnp.where(upper, logf, NEG_BIG)
                ek = jnp.where(upper, NEG_BIG, 0.0)
            prod = _dot_nt(q * jnp.exp(eq), k * jnp.exp(ek))
            shift = int(math.log2(2 * half))
            a = a + jnp.where((t_r >> shift) == (t_c >> shift), prod, 0.0)
            half //= 2
        a_mats.append(jnp.where(t_r == t_c, jnp.sum(q * k, axis=-1, keepdims=True), a))

    hs = [(s, h) for h in range(HG_HEADS) for s in range(nsb)]
    inter = [_dot_nt(qdec_all[s * c:(s + 1) * c, 128 * h:128 * (h + 1)], states[s][h]) for s, h in hs]
    st_new = [states[s][h] * jnp.exp(last_all[s * c:s * c + 1, 128 * h:128 * (h + 1)])
              + _dot_tn(v_all[s * c:(s + 1) * c, 128 * h:128 * (h + 1)],
                        kdec_all[s * c:(s + 1) * c, 128 * h:128 * (h + 1)]) for s, h in hs]
    outs = []
    for h in range(HG_HEADS):
        sl = slice(128 * h, 128 * (h + 1))
        parts = inter[h * nsb:(h + 1) * nsb]
        o = _dot(a_mats[h], v_all[:, sl]) + (jnp.concatenate(parts, axis=0) if nsb > 1 else parts[0])
        o = o * lax.rsqrt(jnp.mean(o * o, axis=-1, keepdims=True) + NORM_EPS) * gn_ref[...]
        outs.append(o * _silu(g_ref[:, sl]))
    o_ref[...] = jnp.concatenate(outs, axis=1).astype(o_ref.dtype)
    for (s, h), st in zip(hs, st_new):
        if has_init:
            sout_ref[s, h] = st.T
        else:
            st_sc[h] = st

    if not has_init:
        @pl.when(ci == pl.num_programs(1) - 1)
        def _():
            for h in range(HG_HEADS):
                sout_ref[0, h] = st_sc[h].T


def _hgrn(proj, lb_raw, g_norm, s0, layer, nseq, t, c, nsb, valid):
    nc = t // c
    has_init = s0 is not None
    assert (nc == 1) if has_init else (nsb == 1)
    rows = nsb * c
    kern = functools.partial(_hgrn_kernel, layer=layer, c=c, nsb=nsb, valid=valid, has_init=has_init)

    def col(cb):
        return pl.BlockSpec((rows, 1024), lambda b, ci: (b * nc + ci, cb))

    in_specs = [pl.BlockSpec((DEPTH, 1024), lambda b, ci: (0, 0)),
                pl.BlockSpec((1, HG_D), lambda b, ci: (0, 0)),
                col(COL_HGQ), col(COL_HGF), col(COL_HGI), col(COL_GHG)]
    args = [lb_raw, g_norm.reshape(1, HG_D), proj, proj, proj, proj]
    if has_init:
        in_specs.append(pl.BlockSpec((None, nsb, HG_HEADS, HG_D, HG_D), lambda b, ci: (layer, b, 0, 0, 0)))
        args.append(s0)
    return pl.pallas_call(
        kern,
        grid=(nseq // nsb, nc),
        in_specs=in_specs,
        out_specs=[pl.BlockSpec((rows, 1024), lambda b, ci: (b * nc + ci, 0)),
                   pl.BlockSpec((nsb, HG_HEADS, HG_D, HG_D), lambda b, ci: (b, 0, 0, 0))],
        out_shape=[jax.ShapeDtypeStruct((nseq * t, 1024), BF16),
                   jax.ShapeDtypeStruct((nseq, HG_HEADS, HG_D, HG_D), F32)],
        scratch_shapes=[] if has_init else [pltpu.VMEM((HG_HEADS, HG_D, HG_D), F32)],
        compiler_params=_cparams(("parallel", "arbitrary")),
        name="hgrn",
    )(*args)


def _rwkv_kernel(xr_ref, xk_ref, xv_ref, xs_ref, g_ref, mu_ref, w0_ref, wup_ref, a0_ref, aup_ref,
                 kk_ref, ka_ref, rk_ref, gnw_ref, gnb_ref, *rest, c, nsb, valid, has_init):
    if has_init:
        sh0_ref, s0_ref, o_ref, sout_ref = rest
    else:
        o_ref, sout_ref, s_sc, carry_sc = rest
        ci = pl.program_id(1)
    npair = RW_HEADS // 2
    rows = nsb * c
    lc = int(math.log2(c))
    w = RW_WIDTH

    if not has_init:
        @pl.when(ci == 0)
        def _():
            carry_sc[...] = jnp.zeros(carry_sc.shape, F32)
            s_sc[...] = jnp.zeros(s_sc.shape, F32)

    t_col = _iota((rows, 1), 0)
    first = (t_col & (c - 1)) == 0
    if has_init:
        sh0 = jnp.broadcast_to(sh0_ref[...], (nsb, c, RW_SHIFT_PAD)).reshape(rows, RW_SHIFT_PAD)

    def shifted(x, lo, hi):
        before = sh0[:, lo:hi] if has_init else carry_sc[0:1, lo:hi]
        prev = jnp.where(first, before, pltpu.roll(x, 1, 0))
        return x + mu_ref[:, lo:hi] * (prev - x)

    xr, xk, xv = xr_ref[...], xk_ref[...], xv_ref[...]
    xs = xs_ref[...]
    xw = xs[:, 768:896]
    xa = xs[:, 896:1024]
    r = shifted(xr, 0, w)
    k = shifted(xk, w, 2 * w)
    v = shifted(xv, 2 * w, 3 * w)
    wl = shifted(xw, 3 * w, 3 * w + 128)
    al = shifted(xa, 3 * w + 128, 3 * w + 256)
    if not has_init:
        carry_sc[0:1, 0:w] = xr[c - 1:c, :]
        carry_sc[0:1, w:2 * w] = xk[c - 1:c, :]
        carry_sc[0:1, 2 * w:3 * w] = xv[c - 1:c, :]
        carry_sc[0:1, 3 * w:3 * w + 128] = xw[c - 1:c, :]
        carry_sc[0:1, 3 * w + 128:3 * w + 256] = xa[c - 1:c, :]

    zw = w0_ref[...] + _dot(jnp.tanh(wl), wup_ref[...])
    log_w = -math.exp(-0.5) * jax.nn.sigmoid(zw)
    ag = jax.nn.sigmoid(a0_ref[...] + _dot(al, aup_ref[...]))

    seg_b = (_iota((128, 128), 0) >> 6) == (_iota((128, 128), 1) >> 6)
    seg = seg_b.astype(BF16)

    def seg_sum(x):
        return jnp.concatenate(
            [jnp.dot(x[:, 128 * p:128 * (p + 1)].astype(BF16), seg, preferred_element_type=F32)
             for p in range(npair)], axis=1)

    kk = k * kk_ref[...]
    kk = kk * lax.rsqrt(jnp.maximum(seg_sum(kk * kk), 1e-24))
    kf = k * (1.0 + (ag - 1.0) * ka_ref[...])
    if valid < c:
        live = (t_col & (c - 1)) < valid
        kk = jnp.where(live, kk, 0.0)
        kf = jnp.where(live, kf, 0.0)
        log_w = jnp.where(live, log_w, 0.0)

    t_r = _iota((rows, rows), 0)
    t_c = _iota((rows, rows), 1)
    tri = ((t_c <= t_r) & ((t_r >> lc) == (t_c >> lc))).astype(F32)
    eye = (t_c == t_r).astype(F32)
    cum = _dot_split(tri, log_w)
    e_in = jnp.exp(cum)
    e_out = jnp.exp(-cum)
    at_all = -kk * jnp.exp(cum - log_w)
    bt_all = kk * ag * e_out
    kt_all = kf * e_out
    rt_all = r * e_in
    r2 = 2 * rows
    a_r = _iota((r2, r2), 0)
    a_c = _iota((r2, r2), 1)
    tr = a_r & (rows - 1)
    tc = a_c & (rows - 1)
    m4 = ((tr >> lc) == (tc >> lc)) & ((tc < tr) | ((a_r >= rows) & (tc == tr)))
    m4_right = m4 & (a_c >= rows)
    head_lo = _iota((1, 128), 1) < RW_HEAD
    nsteps = 0
    cover = 2
    while cover < valid:
        cover *= 2
        nsteps += 1

    def pair(x, p):
        return x[:, 128 * p:128 * (p + 1)]

    if has_init:
        z64 = jnp.zeros((RW_HEAD, RW_HEAD), F32)
        states = [[jnp.concatenate([jnp.concatenate([s0_ref[s, 2 * p], z64], axis=1),
                                    jnp.concatenate([z64, s0_ref[s, 2 * p + 1]], axis=1)], axis=0)
                   for p in range(npair)] for s in range(nsb)]
    else:
        states = [[s_sc[p] for p in range(npair)]]

    ar_at, ar_rt = [], []
    for p in range(npair):
        at_p, rt_p = pair(at_all, p), pair(rt_all, p)
        pa, pr = [], []
        for s in range(nsb):
            rs = slice(s * c, (s + 1) * c)
            ar = _dot_nt(jnp.concatenate([at_p[rs], rt_p[rs]], axis=0), states[s][p])
            pa.append(ar[0:c])
            pr.append(ar[c:2 * c])
        ar_at.append(jnp.concatenate(pa, axis=0) if nsb > 1 else pa[0])
        ar_rt.append(jnp.concatenate(pr, axis=0) if nsb > 1 else pr[0])

    heads = [(p, hh) for p in range(npair) for hh in range(2)]
    hmask = [head_lo if hh == 0 else jnp.logical_not(head_lo) for _, hh in heads]
    labs, mrbs, rhss, ykv = [], [], [], []
    for (p, hh), hm in zip(heads, hmask):
        lhs = jnp.concatenate([jnp.where(hm, pair(at_all, p), 0.0), jnp.where(hm, pair(rt_all, p), 0.0)], axis=0)
        rhs_t = jnp.concatenate([pair(bt_all, p), pair(kt_all, p)], axis=0)
        prod = _dot_nt(lhs, rhs_t)
        vp = pair(v, p)
        lkv = _dot(jnp.where(m4_right, prod, 0.0), jnp.concatenate([vp, vp], axis=0))
        m4p = jnp.where(m4, prod, 0.0)
        labs.append(m4p[0:rows, 0:rows])
        mrbs.append(m4p[rows:r2, 0:rows])
        rhss.append(jnp.where(hm, ar_at[p] + lkv[0:rows], 0.0))
        ykv.append(jnp.where(hm, lkv[rows:r2], 0.0))

    xs_ = [eye + lab for lab in labs]
    pws = labs
    for _ in range(nsteps):
        pws = [_dot(pw, pw) for pw in pws]
        xs_ = [x + _dot(x, pw) for x, pw in zip(xs_, pws)]

    us = [_dot(x, rhs) for x, rhs in zip(xs_, rhss)]
    ys = [_dot(mrb, u) + yk for mrb, u, yk in zip(mrbs, us, ykv)]
    u_pair = [us[2 * p] + us[2 * p + 1] for p in range(npair)]
    y = jnp.concatenate([ar_rt[p] + ys[2 * p] + ys[2 * p + 1] for p in range(npair)], axis=1)

    for p in range(npair):
        bt_p, kt_p, vp, up = pair(bt_all, p), pair(kt_all, p), pair(v, p), u_pair[p]
        for s in range(nsb):
            rs = slice(s * c, (s + 1) * c)
            pc = pair(e_in, p)[s * c + c - 1:s * c + c, :]
            s_new = states[s][p] * pc + _dot_tn(jnp.concatenate([up[rs], vp[rs]], axis=0),
                                                jnp.concatenate([bt_p[rs] * pc, kt_p[rs] * pc], axis=0))
            s_new = jnp.where(seg_b, s_new, 0.0)
            if has_init:
                sout_ref[s, 2 * p] = s_new[0:RW_HEAD, 0:RW_HEAD]
                sout_ref[s, 2 * p + 1] = s_new[RW_HEAD:128, RW_HEAD:128]
            else:
                s_sc[p] = s_new

    mean = seg_sum(y) * (1.0 / RW_HEAD)
    d = y - mean
    var = seg_sum(d * d) * (1.0 / RW_HEAD)
    yn = d * lax.rsqrt(var + GN_EPS) * gnw_ref[...] + gnb_ref[...]
    yn = yn + seg_sum(r * kf * rk_ref[...]) * v
    o_ref[...] = (yn * _silu(g_ref[...])).astype(o_ref.dtype)

    if not has_init:
        @pl.when(ci == pl.num_programs(1) - 1)
        def _():
            for p in range(npair):
                s_bd = s_sc[p]
                sout_ref[0, 2 * p] = s_bd[0:RW_HEAD, 0:RW_HEAD]
                sout_ref[0, 2 * p + 1] = s_bd[RW_HEAD:128, RW_HEAD:128]


def _rwkv(proj, small, prm, sh0, s0, nseq, t, c, nsb, valid):
    nc = t // c
    has_init = s0 is not None
    assert (nc == 1) if has_init else (nsb == 1)
    rows = nsb * c
    kern = functools.partial(_rwkv_kernel, c=c, nsb=nsb, valid=valid, has_init=has_init)

    def col(cb):
        return pl.BlockSpec((rows, 1024), lambda b, ci: (b * nc + ci, cb))

    def const(shape):
        return pl.BlockSpec(shape, lambda b, ci: (0,) * len(shape))

    in_specs = [col(COL_RWR), col(COL_RWK), col(COL_RWV), col(0), col(COL_GRW),
                const((1, RW_SHIFT_PAD)), const((1, 1024)), const((128, 1024)), const((1, 1024)),
                const((128, 1024)), const((1, 1024)), const((1, 1024)), const((1, 1024)),
                const((1, 1024)), const((1, 1024))]
    args = [proj, proj, proj, small, proj, prm["mu"], prm["w0"], prm["w_up"], prm["a0"], prm["a_up"],
            prm["k_k"], prm["k_a"], prm["r_k"], prm["gn_w"], prm["gn_b"]]
    scratch = []
    if has_init:
        in_specs += [pl.BlockSpec((nsb, 1, RW_SHIFT_PAD), lambda b, ci: (b, 0, 0)),
                     pl.BlockSpec((nsb, RW_HEADS, RW_HEAD, RW_HEAD), lambda b, ci: (b, 0, 0, 0))]
        args += [sh0, s0]
    else:
        scratch = [pltpu.VMEM((RW_HEADS // 2, 128, 128), F32), pltpu.VMEM((8, RW_SHIFT_PAD), F32)]
    return pl.pallas_call(
        kern,
        grid=(nseq // nsb, nc),
        in_specs=in_specs,
        out_specs=[pl.BlockSpec((rows, 1024), lambda b, ci: (b * nc + ci, 0)),
                   pl.BlockSpec((nsb, RW_HEADS, RW_HEAD, RW_HEAD), lambda b, ci: (b, 0, 0, 0))],
        out_shape=[jax.ShapeDtypeStruct((nseq * t, 1024), BF16),
                   jax.ShapeDtypeStruct((nseq, RW_HEADS, RW_HEAD, RW_HEAD), F32)],
        scratch_shapes=scratch,
        compiler_params=_cparams(("parallel", "arbitrary")),
        name="rwkv",
    )(*args)


def _out_proj_kernel(x_ref, m_ref, hg_ref, rw_ref, w_ref, g_ref, o_ref):
    acc = _dot(m_ref[...], w_ref[0:2048, :])
    acc = acc + _dot(hg_ref[...], w_ref[2048:3072, :])
    acc = acc + _dot(rw_ref[...], w_ref[3072:4096, :])
    y = acc * lax.rsqrt(jnp.mean(acc * acc, axis=-1, keepdims=True) + NORM_EPS) * g_ref[...]
    o_ref[...] = x_ref[...] + y


def _out_proj(x, m_mla, m_hg, m_rw, w_out, post_norm, tm=256):
    n = x.shape[0]
    return pl.pallas_call(
        _out_proj_kernel,
        grid=(n // tm,),
        in_specs=[pl.BlockSpec((tm, 2048), lambda i: (i, 0)),
                  pl.BlockSpec((tm, 2048), lambda i: (i, 0)),
                  pl.BlockSpec((tm, 1024), lambda i: (i, 0)),
                  pl.BlockSpec((tm, 1024), lambda i: (i, 0)),
                  pl.BlockSpec((4096, 2048), lambda i: (0, 0), pipeline_mode=pl.Buffered(1)),
                  pl.BlockSpec((1, 2048), lambda i: (0, 0))],
        out_specs=pl.BlockSpec((tm, 2048), lambda i: (i, 0)),
        out_shape=jax.ShapeDtypeStruct((n, 2048), F32),
        compiler_params=_cparams(("parallel",)),
        name="out_proj",
    )(x, m_mla, m_hg, m_rw, w_out, post_norm.reshape(1, -1))


def _prep_w_small(w_in_t):
    kr = w_in_t[:, 768:832]
    kr_sw = jnp.concatenate([-kr[:, 32:], kr[:, :32]], axis=1)
    wl = w_in_t[:, RW_X_ROW + 3072:RW_X_ROW + 3136]
    al = w_in_t[:, RW_X_ROW + 3136:RW_X_ROW + 3200]
    return jnp.concatenate([w_in_t[:, 0:832], wl, kr_sw, al], axis=1)


def _prep_wq(w_q_up):
    d, r, h, _ = w_q_up.shape
    nope = w_q_up[..., :MLA_NOPE].reshape(d, r, h * MLA_NOPE)
    pe = w_q_up[..., MLA_NOPE:]
    pe_sw = jnp.concatenate([-pe[..., 32:], pe[..., :32]], axis=-1)
    pad = lambda a: jnp.pad(a, ((0, 0), (0, 0), (0, 0), (0, 64))).reshape(d, r, h * 128)
    return jnp.concatenate([nope, pad(pe), pad(pe_sw)], axis=-1).astype(BF16)


def _reorder_shift(v):
    z = jnp.zeros(v.shape[:-1] + (64,), v.dtype)
    return jnp.concatenate([v[..., :3072], z, v[..., 3072:3136], z, v[..., 3136:3200]], axis=-1)


def kernel(x_prompt, x_sample, cache_mla, page_table, state_hgrn, state_rwkv, state_rwkv_shift, pre_norm, post_norm, w_in, w_out, mla_q_norm, mla_kv_norm, mla_w_q_up, mla_w_kv_up, hg_lower_bound, hg_g_norm, rw_mu, rw_w0, rw_w_up, rw_a0, rw_a_up, rw_k_k, rw_k_a, rw_r_k, rw_gn_w, rw_gn_b):
    nb, t, d = x_prompt.shape
    db, dt, _ = x_sample.shape
    n_pages = page_table.shape[1]
    past_len = n_pages * PAGE
    sp = SAMPLE_PAD

    w_in_t = jnp.swapaxes(w_in, 1, 2)
    w_small_t = _prep_w_small(w_in_t)
    wq_r = _prep_wq(mla_w_q_up)
    wuk = jnp.transpose(mla_w_kv_up[..., :MLA_NOPE], (0, 2, 3, 1)).astype(BF16)
    wuv = jnp.transpose(mla_w_kv_up[..., MLA_NOPE:], (0, 2, 1, 3)).astype(BF16)
    w_out_b = w_out.astype(BF16)
    zpad = jnp.zeros((DEPTH, 64, RW_WIDTH), F32)
    w_up_p = jnp.concatenate([zpad, rw_w_up], axis=1).astype(BF16)
    a_up_p = jnp.concatenate([zpad, rw_a_up], axis=1).astype(BF16)
    mu_r = _reorder_shift(rw_mu)
    sh0_r = _reorder_shift(state_rwkv_shift).reshape(DEPTH, db, 1, RW_SHIFT_PAD)
    half = MLA_ROPE // 2
    inv = ROPE_BASE ** (-jnp.arange(half, dtype=F32) / half)
    inv128 = jnp.tile(inv, 4).reshape(1, 128)
    pos_p = jnp.tile(jnp.arange(t, dtype=F32), nb).reshape(nb * t, 1)
    pos_s = jnp.tile(past_len + jnp.arange(sp, dtype=F32), db).reshape(db * sp, 1)
    pt_flat = page_table.reshape(-1)
    cache_t = jnp.swapaxes(cache_mla, 2, 3)

    xp = x_prompt.reshape(nb * t, d)
    xs = jnp.pad(x_sample, ((0, 0), (0, sp - dt), (0, 0))).reshape(db * sp, d)

    rows_p, rows_s, hg_p, hg_s, rw_p, rw_s, sh_p, sh_s = [], [], [], [], [], [], [], []
    for l in range(DEPTH):
        prm = {"mu": mu_r[l:l + 1], "w0": rw_w0[l:l + 1], "w_up": w_up_p[l], "a0": rw_a0[l:l + 1],
               "a_up": a_up_p[l], "k_k": rw_k_k[l:l + 1], "k_a": rw_k_a[l:l + 1],
               "r_k": rw_r_k[l].reshape(1, RW_WIDTH), "gn_w": rw_gn_w[l:l + 1], "gn_b": rw_gn_b[l:l + 1]}

        h = _rms_cast(xp, pre_norm[l])
        small = _in_proj(h, w_small_t, l, (0,))
        proj = _in_proj(h, w_in_t, l, MAIN_W_ROWS)
        qc, rows, kc = _mla_prep(small, pos_p, inv128, mla_q_norm[l], mla_kv_norm[l], wq_r[l], wuk[l])
        m_mla = _attn_prompt(qc, kc, proj, wuv[l], nb, t)
        m_hg, hg_state = _hgrn(proj, hg_lower_bound, hg_g_norm[l], None, l, nb, t, 128, 1, 128)
        m_rw, rw_state = _rwkv(proj, small, prm, None, None, nb, t, 128, 1, 128)
        rw_off = COL_RWR * 1024
        sh = jnp.concatenate([proj.reshape(nb, t, D_MAIN)[:, t - 1, rw_off:rw_off + 3072],
                              small.reshape(nb, t, 1024)[:, t - 1, 832:896],
                              small.reshape(nb, t, 1024)[:, t - 1, 960:1024]], axis=-1)
        xp = _out_proj(xp, m_mla, m_hg, m_rw, w_out_b[l], post_norm[l])
        rows_p.append(rows.reshape(nb, t, MLA_CACHE)); hg_p.append(hg_state); rw_p.append(rw_state); sh_p.append(sh)

        h = _rms_cast(xs, pre_norm[l])
        small = _in_proj(h, w_small_t, l, (0,))
        proj = _in_proj(h, w_in_t, l, MAIN_W_ROWS)
        qc, rows, kc = _mla_prep(small, pos_s, inv128, mla_q_norm[l], mla_kv_norm[l], wq_r[l], wuk[l])
        q_s = qc.reshape(MLA_HEADS, db, sp, MLA_QK)[:, :, :dt]
        q_s = jnp.transpose(q_s, (1, 0, 2, 3)).reshape(db, MLA_HEADS * dt, MLA_QK)
        newk = jnp.pad(kc.reshape(db, sp, MLA_QK)[:, :dt], ((0, 0), (0, 16 - dt), (0, 0)))
        o_lat = _attn_sample(pt_flat, q_s, newk, cache_t, l, n_pages, dt)
        o_lat = jnp.transpose(o_lat.reshape(db, MLA_HEADS, dt, MLA_KVR), (1, 0, 2, 3))
        o_lat = jnp.pad(o_lat, ((0, 0), (0, 0), (0, sp - dt), (0, 0))).reshape(MLA_HEADS, db * sp, MLA_KVR)
        m_mla = _oproj_gate(o_lat, proj, wuv[l])
        m_hg, hg_state = _hgrn(proj, hg_lower_bound, hg_g_norm[l], state_hgrn, l, db, sp, sp, 8, dt)
        m_rw, rw_state = _rwkv(proj, small, prm, sh0_r[l], state_rwkv[l], db, sp, sp, 8, dt)
        pr = proj.reshape(db, sp, D_MAIN)[:, dt - 1]
        ps = small.reshape(db, sp, 1024)[:, dt - 1]
        sh = jnp.concatenate([pr[:, rw_off:rw_off + 3072], ps[:, 832:896], ps[:, 960:1024]], axis=-1)
        xs = _out_proj(xs, m_mla, m_hg, m_rw, w_out_b[l], post_norm[l])
        rows_s.append(rows.reshape(db, sp, MLA_CACHE)[:, :dt]); hg_s.append(hg_state); rw_s.append(rw_state); sh_s.append(sh)

    y_p = xp.reshape(nb, t, d)
    y_s = xs.reshape(db, sp, d)[:, :dt]
    return (y_p, y_s, jnp.stack(rows_p), jnp.stack(rows_s), jnp.stack(hg_p), jnp.stack(hg_s),
            jnp.stack(rw_p), jnp.stack(rw_s), jnp.stack(sh_p), jnp.stack(sh_s))
```
